```python
import math
import jax, jax.numpy as jnp
from jax import lax
import numpy as np

D_MODEL = 2048
BATCH = 8
SEQ = 2048
DEPTH = 4

N_A_LAYERS = DEPTH // 2
N_B_LAYERS = DEPTH - N_A_LAYERS
N_DENSE_LAYERS = (DEPTH + 1) // 2
N_MOE_LAYERS = DEPTH // 2

CONV_WIDTH = 3

HEAD_DIM = 64
N_HEADS = D_MODEL // HEAD_DIM
N_KV_HEADS = N_HEADS // 8
GROUP = N_HEADS // N_KV_HEADS
KV_DIM = N_KV_HEADS * HEAD_DIM
WINDOW = 128
BLOCK = 128

D_FF = 5632
N_EXPERTS = 8
TOP_K = 2
D_FF_EXPERT = 7168

DEEPNORM_ALPHA = float((2 * DEPTH) ** 0.25)
DEEPNORM_BETA = float((8 * DEPTH) ** -0.25)
LN_EPS = 1e-5
NEG_INF = -1e30

kernel_name = "hybrid_shortconv_swa_sink_yoco_moe"


def layer_norm(x, g, b):
    xf = x.astype(jnp.float32)
    mu = jnp.mean(xf, axis=-1, keepdims=True)
    xc = xf - mu
    var = jnp.mean(xc * xc, axis=-1, keepdims=True)
    y = xc * lax.rsqrt(var + LN_EPS) * g.astype(jnp.float32) + b.astype(jnp.float32)
    return y.astype(x.dtype)


def swiglu(x, w_gate, w_up, w_down):
    return (jax.nn.silu(x @ w_gate) * (x @ w_up)) @ w_down


def short_conv_mixer(x, w_in, conv_w, w_out):
    proj = x @ w_in
    b_gate, c_gate, h = jnp.split(proj, 3, axis=-1)
    u = c_gate * h
    u_pad = jnp.pad(u, ((0, 0), (CONV_WIDTH - 1, 0), (0, 0)))
    seq = u.shape[1]
    conv = (conv_w[0] * u_pad[:, 0:seq]
            + conv_w[1] * u_pad[:, 1:seq + 1]
            + conv_w[2] * u_pad[:, 2:seq + 2])
    return (b_gate * conv) @ w_out


def sliding_window_gqa_sinks(x, k, v, w_q, sinks, w_o):
    bsz, seq, _ = x.shape
    nb = seq // BLOCK
    q = (x @ w_q).reshape(bsz, nb, BLOCK, N_KV_HEADS, GROUP, HEAD_DIM)

    def band(t):
        tp = jnp.pad(t, ((0, 0), (BLOCK, 0), (0, 0), (0, 0)))
        tp = tp.reshape(bsz, nb + 1, BLOCK, N_KV_HEADS, HEAD_DIM)
        return jnp.concatenate([tp[:, :-1], tp[:, 1:]], axis=2)

    kb, vb = band(k), band(v)
    scale = 1.0 / math.sqrt(HEAD_DIM)
    s = jnp.einsum('bnqkgd,bnskd->bnkgqs', q.astype(jnp.float32), kb.astype(jnp.float32)) * scale
    blk = jnp.arange(nb)[:, None, None] * BLOCK
    qpos = blk + jnp.arange(BLOCK)[None, :, None]
    kpos = blk - BLOCK + jnp.arange(2 * BLOCK)[None, None, :]
    mask = (kpos <= qpos) & (qpos - kpos < WINDOW) & (kpos >= 0)
    s = jnp.where(mask[None, :, None, None], s, NEG_INF)
    sink = jnp.broadcast_to(sinks.astype(jnp.float32).reshape(1, 1, N_KV_HEADS, GROUP, 1, 1),
                            s.shape[:-1] + (1,))
    p = jax.nn.softmax(jnp.concatenate([s, sink], axis=-1), axis=-1)[..., :-1]
    o = jnp.einsum('bnkgqs,bnskd->bnqkgd', p, vb.astype(jnp.float32))
    o = o.astype(x.dtype).reshape(bsz, seq, N_HEADS * HEAD_DIM)
    return o @ w_o


def moe_swiglu(x, w_router, w_gate, w_up, w_down):
    shape = x.shape
    xt = x.reshape(-1, shape[-1])
    logits = (xt @ w_router).astype(jnp.float32)
    top_val, top_idx = lax.top_k(logits, TOP_K)
    gates = jax.nn.softmax(top_val, axis=-1)
    combine = jnp.einsum('nk,nke->ne', gates,
                         jax.nn.one_hot(top_idx, N_EXPERTS, dtype=jnp.float32))
    out = jnp.zeros(xt.shape, jnp.float32)
    for e in range(N_EXPERTS):
        out = out + combine[:, e:e + 1] * swiglu(xt, w_gate[e], w_up[e], w_down[e]).astype(jnp.float32)
    return out.astype(x.dtype).reshape(shape)


def setup_inputs(seed: int = 0) -> dict:
    key = jax.random.key(seed)
    ks = jax.random.split(key, 20)
    f32 = jnp.float32
    d_in = D_MODEL ** -0.5
    nrm = lambda k, shape, s: jax.random.normal(k, shape, f32) * s
    return {
        "x": nrm(ks[0], (BATCH, SEQ, D_MODEL), 1.0),
        "conv_w_in": nrm(ks[1], (N_A_LAYERS, D_MODEL, 3 * D_MODEL), d_in),
        "conv_w": nrm(ks[2], (N_A_LAYERS, CONV_WIDTH, D_MODEL), CONV_WIDTH ** -0.5),
        "conv_w_out": nrm(ks[3], (N_A_LAYERS, D_MODEL, D_MODEL), d_in * DEEPNORM_BETA),
        "w_kv": nrm(ks[4], (D_MODEL, 2 * KV_DIM), d_in),
        "attn_w_q": nrm(ks[5], (N_B_LAYERS, D_MODEL, N_HEADS * HEAD_DIM), d_in),
        "attn_sinks": nrm(ks[6], (N_B_LAYERS, N_HEADS), 0.5),
        "attn_w_o": nrm(ks[7], (N_B_LAYERS, N_HEADS * HEAD_DIM, D_MODEL),
                        (N_HEADS * HEAD_DIM) ** -0.5 * DEEPNORM_BETA),
        "ln_g": 1.0 + nrm(ks[8], (DEPTH, 2, D_MODEL), 0.02),
        "ln_b": nrm(ks[9], (DEPTH, 2, D_MODEL), 0.02),
        "ffn_w_gate": nrm(ks[10], (N_DENSE_LAYERS, D_MODEL, D_FF), d_in),
        "ffn_w_up": nrm(ks[11], (N_DENSE_LAYERS, D_MODEL, D_FF), d_in),
        "ffn_w_down": nrm(ks[12], (N_DENSE_LAYERS, D_FF, D_MODEL), D_FF ** -0.5 * DEEPNORM_BETA),
        "moe_w_router": nrm(ks[13], (N_MOE_LAYERS, D_MODEL, N_EXPERTS), d_in),
        "moe_w_gate": nrm(ks[14], (N_MOE_LAYERS, N_EXPERTS, D_MODEL, D_FF_EXPERT), d_in),
        "moe_w_up": nrm(ks[15], (N_MOE_LAYERS, N_EXPERTS, D_MODEL, D_FF_EXPERT), d_in),
        "moe_w_down": nrm(ks[16], (N_MOE_LAYERS, N_EXPERTS, D_FF_EXPERT, D_MODEL),
                          D_FF_EXPERT ** -0.5 * DEEPNORM_BETA),
    }


def reference(x, conv_w_in, conv_w, conv_w_out, w_kv, attn_w_q, attn_sinks, attn_w_o,
              ln_g, ln_b, ffn_w_gate, ffn_w_up, ffn_w_down,
              moe_w_router, moe_w_gate, moe_w_up, moe_w_down):
    bsz, seq, _ = x.shape
    k_shared = v_shared = None
    for l in range(DEPTH):
        if l < N_A_LAYERS:
            mix = short_conv_mixer(x, conv_w_in[l], conv_w[l], conv_w_out[l])
        else:
            j = l - N_A_LAYERS
            mix = sliding_window_gqa_sinks(x, k_shared, v_shared,
                                           attn_w_q[j], attn_sinks[j], attn_w_o[j])
        x = layer_norm(DEEPNORM_ALPHA * x + mix, ln_g[l, 0], ln_b[l, 0])
        if l % 2 == 0:
            i = l // 2
            ff = swiglu(x, ffn_w_gate[i], ffn_w_up[i], ffn_w_down[i])
        else:
            i = l // 2
            ff = moe_swiglu(x, moe_w_router[i], moe_w_gate[i], moe_w_up[i], moe_w_down[i])
        x = layer_norm(DEEPNORM_ALPHA * x + ff, ln_g[l, 1], ln_b[l, 1])
        if l == N_A_LAYERS - 1:
            kv = x @ w_kv
            k_shared = kv[..., :KV_DIM].reshape(bsz, seq, N_KV_HEADS, HEAD_DIM)
            v_shared = kv[..., KV_DIM:].reshape(bsz, seq, N_KV_HEADS, HEAD_DIM)
    return x
```

```python
import functools
import math

import jax
import jax.numpy as jnp
from jax import lax
from jax.experimental import pallas as pl
from jax.experimental.pallas import tpu as pltpu

F32 = jnp.float32
BF16 = jnp.bfloat16

HEAD_DIM = 64
GQA_GROUP = 8
ATTN_BLOCK = 128
TOP_K = 2
LN_EPS = 1e-5
NEG_INF = -1e30

V7X_LANES = 128
V7X_SUBLANES = 8
V7X_VMEM_BYTES = 64 * 1024 * 1024
V7X_VMEM_BUDGET = V7X_VMEM_BYTES - 8 * 1024 * 1024

MOE_TILE = 512


def _tile(dim, target, quantum=V7X_LANES):
    if dim <= target:
        return dim
    t = (target // quantum) * quantum
    while t > quantum and dim % t:
        t -= quantum
    assert dim % t == 0, (dim, target)
    return t


def _params(semantics, vmem_bytes):
    limit = min(V7X_VMEM_BUDGET, max(32 * 1024 * 1024, int(vmem_bytes)))
    return pltpu.CompilerParams(dimension_semantics=semantics, vmem_limit_bytes=limit)


def _dot(a, b):
    return jnp.dot(a, b, preferred_element_type=F32)


def _layer_norm(z, g, b):
    mu = jnp.mean(z, axis=-1, keepdims=True)
    zc = z - mu
    var = jnp.mean(zc * zc, axis=-1, keepdims=True)
    return zc * lax.rsqrt(var + LN_EPS) * g + b


def _matmul_kernel(x_ref, w_ref, o_ref, *, scale):
    acc = _dot(x_ref[...], w_ref[...])
    if scale != 1.0:
        acc = acc * scale
    o_ref[...] = acc.astype(o_ref.dtype)


def _matmul(x, w, *, out_dtype, scale=1.0, name):
    n, k = x.shape
    m = w.shape[1]
    tm = _tile(n, 1024)
    tn = _tile(m, 1024)
    vmem = 2 * (tm * k * 2 + k * tn * 2 + tm * tn * 4) + tm * tn * 4
    return pl.pallas_call(
        functools.partial(_matmul_kernel, scale=scale),
        grid=(m // tn, n // tm),
        in_specs=[pl.BlockSpec((tm, k), lambda j, i: (i, 0)),
                  pl.BlockSpec((k, tn), lambda j, i: (0, j))],
        out_specs=pl.BlockSpec((tm, tn), lambda j, i: (i, j)),
        out_shape=jax.ShapeDtypeStruct((n, m), out_dtype),
        compiler_params=_params(("parallel", "parallel"), vmem),
        name=name,
    )(x, w)


def _conv_gate_kernel(x_ref, wb_ref, wc_ref, wh_ref, cw_ref, o_ref, ubuf_ref, *, tiles_per_seq):
    i = pl.program_id(1)
    tm = x_ref.shape[0]
    pad = V7X_SUBLANES

    @pl.when(i % tiles_per_seq == 0)
    def _():
        ubuf_ref[0:pad, :] = jnp.zeros((pad, ubuf_ref.shape[1]), F32)

    x = x_ref[...]
    u = _dot(x, wc_ref[...]) * _dot(x, wh_ref[...])
    ubuf_ref[pad:pad + tm, :] = u
    u1 = ubuf_ref[pad - 1:pad - 1 + tm, :]
    u2 = ubuf_ref[pad - 2:pad - 2 + tm, :]
    cw = cw_ref[...]
    conv = cw[0:1, :] * u2 + cw[1:2, :] * u1 + cw[2:3, :] * u
    o_ref[...] = (_dot(x, wb_ref[...]) * conv).astype(o_ref.dtype)
    ubuf_ref[0:pad, :] = ubuf_ref[tm:tm + pad, :]


def _conv_gate(xb, w_in, conv_w, *, seq):
    n, d = xb.shape
    tm = _tile(seq, 1024)
    tn = _tile(d, 256)
    nj = d // tn
    vmem = 2 * (tm * d * 2 + 3 * d * tn * 2 + tm * tn * 2) + (tm + 8) * tn * 4 + 6 * tm * tn * 4
    return pl.pallas_call(
        functools.partial(_conv_gate_kernel, tiles_per_seq=seq // tm),
        grid=(nj, n // tm),
        in_specs=[pl.BlockSpec((tm, d), lambda j, i: (i, 0)),
                  pl.BlockSpec((d, tn), lambda j, i: (0, j)),
                  pl.BlockSpec((d, tn), lambda j, i: (0, nj + j)),
                  pl.BlockSpec((d, tn), lambda j, i: (0, 2 * nj + j)),
                  pl.BlockSpec((3, tn), lambda j, i: (0, j))],
        out_specs=pl.BlockSpec((tm, tn), lambda j, i: (i, j)),
        out_shape=jax.ShapeDtypeStruct((n, d), BF16),
        scratch_shapes=[pltpu.VMEM((tm + V7X_SUBLANES, tn), F32)],
        compiler_params=_params(("parallel", "arbitrary"), vmem),
        name="conv_gate",
    )(xb, w_in, w_in, w_in, conv_w)


def _proj_ln_kernel(a_ref, w_ref, res_ref, g_ref, b_ref, of_ref, ob_ref, acc_ref, *, alpha, nk):
    k = pl.program_id(1)
    part = _dot(a_ref[...], w_ref[...])

    def finish(acc):
        y = _layer_norm(alpha * res_ref[...] + acc, g_ref[...], b_ref[...])
        of_ref[...] = y
        ob_ref[...] = y.astype(BF16)

    if nk == 1:
        finish(part)
        return

    @pl.when(k == 0)
    def _():
        acc_ref[...] = part

    @pl.when(jnp.logical_and(k > 0, k < nk - 1))
    def _():
        acc_ref[...] += part

    @pl.when(k == nk - 1)
    def _():
        finish(acc_ref[...] + part)


def _proj_ln(a, w, res, g, b, *, alpha, name):
    n, kdim = a.shape
    d = w.shape[1]
    tm = _tile(n, 512)
    tk = kdim if kdim <= 2048 else _tile(kdim, 1536)
    nk = kdim // tk
    vmem = 2 * (tm * tk * 2 + tk * d * 2 + tm * d * 4 + tm * d * 6) + 3 * tm * d * 4
    return pl.pallas_call(
        functools.partial(_proj_ln_kernel, alpha=alpha, nk=nk),
        grid=(n // tm, nk),
        in_specs=[pl.BlockSpec((tm, tk), lambda i, k: (i, k)),
                  pl.BlockSpec((tk, d), lambda i, k: (k, 0)),
                  pl.BlockSpec((tm, d), lambda i, k: (i, 0)),
                  pl.BlockSpec((1, d), lambda i, k: (0, 0)),
                  pl.BlockSpec((1, d), lambda i, k: (0, 0))],
        out_specs=[pl.BlockSpec((tm, d), lambda i, k: (i, 0)),
                   pl.BlockSpec((tm, d), lambda i, k: (i, 0))],
        out_shape=[jax.ShapeDtypeStruct((n, d), F32), jax.ShapeDtypeStruct((n, d), BF16)],
        scratch_shapes=[pltpu.VMEM((tm, d), F32)],
        compiler_params=_params(("parallel", "arbitrary"), vmem),
        name=name,
    )(a, w, res, g, b)


def _swiglu_up_kernel(x_ref, wg_ref, wu_ref, o_ref):
    x = x_ref[...]
    o_ref[...] = (jax.nn.silu(_dot(x, wg_ref[...])) * _dot(x, wu_ref[...])).astype(o_ref.dtype)


def _swiglu_up(xb, wg, wu):
    n, d = xb.shape
    f = wg.shape[1]
    tm = _tile(n, 1024)
    tf = _tile(f, 512)
    vmem = 2 * (tm * d * 2 + 2 * d * tf * 2 + tm * tf * 2) + 4 * tm * tf * 4
    return pl.pallas_call(
        _swiglu_up_kernel,
        grid=(n // tm, f // tf),
        in_specs=[pl.BlockSpec((tm, d), lambda i, j: (i, 0)),
                  pl.BlockSpec((d, tf), lambda i, j: (0, j)),
                  pl.BlockSpec((d, tf), lambda i, j: (0, j))],
        out_specs=pl.BlockSpec((tm, tf), lambda i, j: (i, j)),
        out_shape=jax.ShapeDtypeStruct((n, f), BF16),
        compiler_params=_params(("parallel", "parallel"), vmem),
        name="swiglu_up",
    )(xb, wg, wu)


def _attn_kernel(sink_ref, q_ref, kp_ref, kc_ref, vp_ref, vc_ref, o_ref, *, n_heads):
    has_prev = pl.program_id(1) > 0
    blk = q_ref.shape[0]
    row = lax.broadcasted_iota(jnp.int32, (blk, blk), 0)
    col = lax.broadcasted_iota(jnp.int32, (blk, blk), 1)
    mask_c = col <= row
    mask_p = jnp.logical_and(col > row, has_prev)
    nt = (((1,), (1,)), ((), ()))
    for h in range(n_heads):
        qs = slice(h * HEAD_DIM, (h + 1) * HEAD_DIM)
        ks = slice((h // GQA_GROUP) * HEAD_DIM, (h // GQA_GROUP + 1) * HEAD_DIM)
        qh = q_ref[:, qs]
        s_c = lax.dot_general(qh, kc_ref[:, ks], nt, preferred_element_type=F32)
        s_p = lax.dot_general(qh, kp_ref[:, ks], nt, preferred_element_type=F32)
        s_c = jnp.where(mask_c, s_c, NEG_INF)
        s_p = jnp.where(mask_p, s_p, NEG_INF)
        sink = sink_ref[h]
        m = jnp.maximum(jnp.maximum(jnp.max(s_c, axis=-1, keepdims=True),
                                    jnp.max(s_p, axis=-1, keepdims=True)), sink)
        p_c = jnp.exp(s_c - m)
        p_p = jnp.exp(s_p - m)
        denom = (jnp.sum(p_c, axis=-1, keepdims=True) + jnp.sum(p_p, axis=-1, keepdims=True)
                 + jnp.exp(sink - m))
        o = _dot(p_c.astype(BF16), vc_ref[:, ks]) + _dot(p_p.astype(BF16), vp_ref[:, ks])
        o_ref[:, qs] = (o / denom).astype(o_ref.dtype)


def _attention(q, kv, sinks, *, bsz, seq):
    n, d = q.shape
    kvd = kv.shape[1] // 2
    nb = seq // ATTN_BLOCK
    cur = lambda b, i: (b * nb + i, 0)
    prev = lambda b, i: (b * nb + jnp.maximum(i - 1, 0), 0)
    cur_v = lambda b, i: (b * nb + i, 1)
    prev_v = lambda b, i: (b * nb + jnp.maximum(i - 1, 0), 1)
    vmem = 2 * (2 * ATTN_BLOCK * d * 2 + 4 * ATTN_BLOCK * kvd * 2) + 64 * ATTN_BLOCK * ATTN_BLOCK * 4
    return pl.pallas_call(
        functools.partial(_attn_kernel, n_heads=d // HEAD_DIM),
        grid=(bsz, nb),
        in_specs=[pl.BlockSpec(memory_space=pltpu.SMEM),
                  pl.BlockSpec((ATTN_BLOCK, d), cur),
                  pl.BlockSpec((ATTN_BLOCK, kvd), prev),
                  pl.BlockSpec((ATTN_BLOCK, kvd), cur),
                  pl.BlockSpec((ATTN_BLOCK, kvd), prev_v),
                  pl.BlockSpec((ATTN_BLOCK, kvd), cur_v)],
        out_specs=pl.BlockSpec((ATTN_BLOCK, d), cur),
        out_shape=jax.ShapeDtypeStruct((n, d), BF16),
        compiler_params=_params(("parallel", "parallel"), vmem),
        name="swa_attention",
    )(sinks, q, kv, kv, kv, kv)


def _router_kernel(x_ref, wr_ref, ri_ref, rg_ref, cnt_ref, carry_ref, *, n_experts):
    i = pl.program_id(0)
    tm = x_ref.shape[0]

    @pl.when(i == 0)
    def _():
        carry_ref[...] = jnp.zeros_like(carry_ref)

    logits = jnp.dot(x_ref[...], wr_ref[...], preferred_element_type=F32,
                     precision=lax.Precision.HIGHEST)
    lane = lax.broadcasted_iota(jnp.int32, logits.shape, 1)
    logits = jnp.where(lane < n_experts, logits, -jnp.inf)
    m1 = jnp.max(logits, axis=-1, keepdims=True)
    i1 = jnp.min(jnp.where(logits == m1, lane, V7X_LANES), axis=-1, keepdims=True)
    rest = jnp.where(lane == i1, -jnp.inf, logits)
    m2 = jnp.max(rest, axis=-1, keepdims=True)
    i2 = jnp.min(jnp.where(rest == m2, lane, V7X_LANES), axis=-1, keepdims=True)
    e2 = jnp.exp(m2 - m1)
    g1 = 1.0 / (1.0 + e2)
    g2 = e2 / (1.0 + e2)

    sel1 = lane == i1
    sel2 = lane == i2
    member = jnp.logical_or(sel1, sel2).astype(BF16)
    r = lax.broadcasted_iota(jnp.int32, (tm, tm), 0)
    c = lax.broadcasted_iota(jnp.int32, (tm, tm), 1)
    before = (c < r).astype(BF16)
    prior = carry_ref[...] + _dot(before, member)
    r1 = jnp.sum(jnp.where(sel1, prior, 0.0), axis=-1, keepdims=True).astype(jnp.int32)
    r2 = jnp.sum(jnp.where(sel2, prior, 0.0), axis=-1, keepdims=True).astype(jnp.int32)
    carry_ref[...] += jnp.sum(member.astype(F32), axis=0, keepdims=True)

    ri_ref[...] = jnp.where(lane == 0, i1, jnp.where(lane == 1, i2,
                            jnp.where(lane == 2, r1, jnp.where(lane == 3, r2, 0))))
    rg_ref[...] = jnp.where(lane == 0, g1, jnp.where(lane == 1, g2, 0.0))
    cnt_ref[...] = carry_ref[...].astype(jnp.int32)


def _router(x, w_router_padded, *, n_experts):
    n, d = x.shape
    tm = _tile(n, 512)
    vmem = 2 * (tm * d * 4 + d * V7X_LANES * 4 + 2 * tm * V7X_LANES * 4) + 8 * tm * tm * 4
    return pl.pallas_call(
        functools.partial(_router_kernel, n_experts=n_experts),
        grid=(n // tm,),
        in_specs=[pl.BlockSpec((tm, d), lambda i: (i, 0)),
                  pl.BlockSpec((d, V7X_LANES), lambda i: (0, 0))],
        out_specs=[pl.BlockSpec((tm, V7X_LANES), lambda i: (i, 0)),
                   pl.BlockSpec((tm, V7X_LANES), lambda i: (i, 0)),
                   pl.BlockSpec((1, V7X_LANES), lambda i: (0, 0))],
        out_shape=[jax.ShapeDtypeStruct((n, V7X_LANES), jnp.int32),
                   jax.ShapeDtypeStruct((n, V7X_LANES), F32),
                   jax.ShapeDtypeStruct((1, V7X_LANES), jnp.int32)],
        scratch_shapes=[pltpu.VMEM((1, V7X_LANES), F32)],
        compiler_params=_params(("arbitrary",), vmem),
        name="moe_router",
    )(x, w_router_padded)


def _row_copy(src_hbm, src_row, dst_hbm, dst_row, sem):
    return pltpu.make_async_copy(src_hbm.at[pl.ds(src_row, 1), :], dst_hbm.at[pl.ds(dst_row, 1), :], sem)


def _dispatch_kernel(slot_ref, x_hbm, init_hbm, xs_hbm, sem, *, chunk):
    del init_hbm
    base = pl.program_id(0) * chunk

    def start(r, carry):
        for k in range(TOP_K):
            _row_copy(x_hbm, base + r, xs_hbm, slot_ref[0, 0, TOP_K * r + k], sem).start()
        return carry

    def wait(r, carry):
        for k in range(TOP_K):
            _row_copy(x_hbm, base + r, xs_hbm, slot_ref[0, 0, TOP_K * r + k], sem).wait()
        return carry

    lax.fori_loop(0, chunk, start, 0)
    lax.fori_loop(0, chunk, wait, 0)


def _dispatch(x, slots, *, n_slots):
    n, d = x.shape
    chunk = _tile(n, 512)
    slots3 = slots.reshape(n // chunk, 1, TOP_K * chunk)
    init = jnp.zeros((n_slots, d), x.dtype)
    return pl.pallas_call(
        functools.partial(_dispatch_kernel, chunk=chunk),
        grid=(n // chunk,),
        in_specs=[pl.BlockSpec((1, 1, TOP_K * chunk), lambda i: (i, 0, 0), memory_space=pltpu.SMEM),
                  pl.BlockSpec(memory_space=pl.ANY),
                  pl.BlockSpec(memory_space=pl.ANY)],
        out_specs=pl.BlockSpec(memory_space=pl.ANY),
        out_shape=jax.ShapeDtypeStruct((n_slots, d), x.dtype),
        scratch_shapes=[pltpu.SemaphoreType.DMA(())],
        input_output_aliases={2: 0},
        compiler_params=_params(("arbitrary",), 0),
        name="moe_dispatch",
    )(slots3, x, init)


def _moe_up_kernel(te_ref, first_ref, nused_ref, xs_ref, wg_ref, wu_ref, o_ref, wgb_ref, wub_ref):
    del te_ref
    t = pl.program_id(1)

    @pl.when(first_ref[t] == 1)
    def _():
        wgb_ref[...] = wg_ref[...].astype(BF16)
        wub_ref[...] = wu_ref[...].astype(BF16)

    @pl.when(t < nused_ref[0])
    def _():
        x = xs_ref[...].astype(BF16)
        o_ref[...] = (jax.nn.silu(_dot(x, wgb_ref[...])) * _dot(x, wub_ref[...])).astype(o_ref.dtype)

    @pl.when(t >= nused_ref[0])
    def _():
        o_ref[...] = jnp.zeros_like(o_ref)


def _moe_up(xs, w_gate, w_up, tile_expert, tile_first, n_used):
    s, d = xs.shape
    f = w_gate.shape[2]
    tm = MOE_TILE
    tf = _tile(f, 512)
    vmem = 2 * (tm * d * 4 + 2 * d * tf * 4 + tm * tf * 2) + 2 * d * tf * 2 + tm * d * 2 + 4 * tm * tf * 4
    grid_spec = pltpu.PrefetchScalarGridSpec(
        num_scalar_prefetch=3,
        grid=(f // tf, s // tm),
        in_specs=[pl.BlockSpec((tm, d), lambda j, t, te, fi, nu: (t, 0)),
                  pl.BlockSpec((None, d, tf), lambda j, t, te, fi, nu: (te[t], 0, j)),
                  pl.BlockSpec((None, d, tf), lambda j, t, te, fi, nu: (te[t], 0, j))],
        out_specs=pl.BlockSpec((tm, tf), lambda j, t, te, fi, nu: (t, j)),
        scratch_shapes=[pltpu.VMEM((d, tf), BF16), pltpu.VMEM((d, tf), BF16)],
    )
    return pl.pallas_call(
        _moe_up_kernel,
        grid_spec=grid_spec,
        out_shape=jax.ShapeDtypeStruct((s, f), BF16),
        compiler_params=_params(("arbitrary", "arbitrary"), vmem),
        name="moe_up",
    )(tile_expert, tile_first, n_used, xs, w_gate, w_up)


def _moe_down_kernel(te_ref, first_ref, nused_ref, h_ref, wd_ref, o_ref, wdb_ref):
    del te_ref
    t = pl.program_id(1)

    @pl.when(first_ref[t] == 1)
    def _():
        wdb_ref[...] = wd_ref[...].astype(BF16)

    @pl.when(t < nused_ref[0])
    def _():
        o_ref[...] = _dot(h_ref[...], wdb_ref[...])

    @pl.when(t >= nused_ref[0])
    def _():
        o_ref[...] = jnp.zeros_like(o_ref)


def _moe_down(h, w_down, tile_expert, tile_first, n_used):
    s, f = h.shape
    d = w_down.shape[2]
    tm = MOE_TILE
    tn = _tile(d, 256)
    vmem = 2 * (tm * f * 2 + f * tn * 4 + tm * tn * 4) + f * tn * 2 + 2 * tm * tn * 4
    grid_spec = pltpu.PrefetchScalarGridSpec(
        num_scalar_prefetch=3,
        grid=(d // tn, s // tm),
        in_specs=[pl.BlockSpec((tm, f), lambda j, t, te, fi, nu: (t, 0)),
                  pl.BlockSpec((None, f, tn), lambda j, t, te, fi, nu: (te[t], 0, j))],
        out_specs=pl.BlockSpec((tm, tn), lambda j, t, te, fi, nu: (t, j)),
        scratch_shapes=[pltpu.VMEM((f, tn), BF16)],
    )
    return pl.pallas_call(
        _moe_down_kernel,
        grid_spec=grid_spec,
        out_shape=jax.ShapeDtypeStruct((s, d), F32),
        compiler_params=_params(("arbitrary", "arbitrary"), vmem),
        name="moe_down",
    )(tile_expert, tile_first, n_used, h, w_down)


def _combine_ln_kernel(slot_ref, y_hbm, rg_ref, res_ref, g_ref, b_ref, of_ref, ob_ref, buf_ref, sem,
                       *, alpha):
    tm = res_ref.shape[0]

    def copies(r):
        return [pltpu.make_async_copy(y_hbm.at[pl.ds(slot_ref[0, 0, TOP_K * r + k], 1), :],
                                      buf_ref.at[k, pl.ds(r, 1), :], sem) for k in range(TOP_K)]

    def start(r, carry):
        for cp in copies(r):
            cp.start()
        return carry

    def wait(r, carry):
        for cp in copies(r):
            cp.wait()
        return carry

    lax.fori_loop(0, tm, start, 0)
    lax.fori_loop(0, tm, wait, 0)
    gates = rg_ref[...]
    ff = gates[:, 0:1] * buf_ref[0] + gates[:, 1:2] * buf_ref[1]
    y = _layer_norm(alpha * res_ref[...] + ff, g_ref[...], b_ref[...])
    of_ref[...] = y
    ob_ref[...] = y.astype(BF16)


def _combine_ln(y, slots, gates, res, g, b, *, alpha):
    n, d = res.shape
    tm = _tile(n, 256)
    slots3 = slots.reshape(n // tm, 1, TOP_K * tm)
    vmem = 2 * (tm * V7X_LANES * 4 + tm * d * 4 + tm * d * 6) + TOP_K * tm * d * 4 + 3 * tm * d * 4
    return pl.pallas_call(
        functools.partial(_combine_ln_kernel, alpha=alpha),
        grid=(n // tm,),
        in_specs=[pl.BlockSpec((1, 1, TOP_K * tm), lambda i: (i, 0, 0), memory_space=pltpu.SMEM),
                  pl.BlockSpec(memory_space=pl.ANY),
                  pl.BlockSpec((tm, V7X_LANES), lambda i: (i, 0)),
                  pl.BlockSpec((tm, d), lambda i: (i, 0)),
                  pl.BlockSpec((1, d), lambda i: (0, 0)),
                  pl.BlockSpec((1, d), lambda i: (0, 0))],
        out_specs=[pl.BlockSpec((tm, d), lambda i: (i, 0)),
                   pl.BlockSpec((tm, d), lambda i: (i, 0))],
        out_shape=[jax.ShapeDtypeStruct((n, d), F32), jax.ShapeDtypeStruct((n, d), BF16)],
        scratch_shapes=[pltpu.VMEM((TOP_K, tm, d), F32), pltpu.SemaphoreType.DMA(())],
        compiler_params=_params(("arbitrary",), vmem),
        name="moe_combine_ln",
    )(slots3, y, gates, res, g, b)


def _moe_layer(x, w_router, w_gate, w_up, w_down, g, b, *, alpha):
    n, d = x.shape
    n_experts = w_router.shape[1]
    tm = MOE_TILE
    n_tiles = (TOP_K * n) // tm + n_experts
    wr = jnp.pad(w_router, ((0, 0), (0, V7X_LANES - n_experts)))
    ri, rg, cnt = _router(x, wr, n_experts=n_experts)

    counts = cnt[0, :n_experts]
    tiles_per = (counts + tm - 1) // tm
    tile_end = jnp.cumsum(tiles_per)
    n_used = tile_end[-1]
    group_start = (tile_end - tiles_per) * tm
    tile_id = jnp.arange(n_tiles, dtype=jnp.int32)
    tile_expert = jnp.minimum(jnp.searchsorted(tile_end, tile_id, side="right"),
                              n_experts - 1).astype(jnp.int32)
    tile_expert = jnp.where(tile_id < n_used, tile_expert, tile_expert[jnp.maximum(n_used - 1, 0)])
    tile_first = jnp.concatenate([jnp.ones((1,), jnp.int32),
                                  (tile_expert[1:] != tile_expert[:-1]).astype(jnp.int32)])
    slots = (group_start[ri[:, :TOP_K]] + ri[:, TOP_K:2 * TOP_K]).astype(jnp.int32).reshape(-1)
    n_used = n_used.astype(jnp.int32).reshape(1)

    xs = _dispatch(x, slots, n_slots=n_tiles * tm)
    h = _moe_up(xs, w_gate, w_up, tile_expert, tile_first, n_used)
    y = _moe_down(h, w_down, tile_expert, tile_first, n_used)
    return _combine_ln(y, slots, rg, x, g, b, alpha=alpha)


def kernel(x, conv_w_in, conv_w, conv_w_out, w_kv, attn_w_q, attn_sinks, attn_w_o, ln_g, ln_b,
           ffn_w_gate, ffn_w_up, ffn_w_down, moe_w_router, moe_w_gate, moe_w_up, moe_w_down):
    bsz, seq, d = x.shape
    n = bsz * seq
    depth = ln_g.shape[0]
    n_a = conv_w_in.shape[0]
    alpha = float((2 * depth) ** 0.25)
    assert seq % ATTN_BLOCK == 0 and d % (HEAD_DIM * GQA_GROUP) == 0
    assert w_kv.shape[1] == 2 * d // GQA_GROUP

    xf = x.reshape(n, d)
    xb = xf.astype(BF16)
    kv = None
    for l in range(depth):
        g0, b0 = ln_g[l, 0].reshape(1, d), ln_b[l, 0].reshape(1, d)
        g1, b1 = ln_g[l, 1].reshape(1, d), ln_b[l, 1].reshape(1, d)
        if l < n_a:
            gated = _conv_gate(xb, conv_w_in[l].astype(BF16), conv_w[l], seq=seq)
            xf, xb = _proj_ln(gated, conv_w_out[l].astype(BF16), xf, g0, b0, alpha=alpha,
                              name="conv_out_ln")
        else:
            j = l - n_a
            q = _matmul(xb, attn_w_q[j].astype(BF16), out_dtype=BF16,
                        scale=1.0 / math.sqrt(HEAD_DIM), name="attn_q")
            o = _attention(q, kv, attn_sinks[j], bsz=bsz, seq=seq)
            xf, xb = _proj_ln(o, attn_w_o[j].astype(BF16), xf, g0, b0, alpha=alpha, name="attn_out_ln")
        i = l // 2
        if l % 2 == 0:
            h = _swiglu_up(xb, ffn_w_gate[i].astype(BF16), ffn_w_up[i].astype(BF16))
            xf, xb = _proj_ln(h, ffn_w_down[i].astype(BF16), xf, g1, b1, alpha=alpha, name="ffn_down_ln")
        else:
            xf, xb = _moe_layer(xf, moe_w_router[i], moe_w_gate[i], moe_w_up[i], moe_w_down[i],
                                g1, b1, alpha=alpha)
        if l == n_a - 1:
            kv = _matmul(xb, w_kv.astype(BF16), out_dtype=BF16, name="kv_proj")
    return xf.reshape(bsz, seq, d)
```

```python
import functools
import math

import jax
import jax.numpy as jnp
from jax import lax
from jax.experimental import pallas as pl
from jax.experimental.pallas import tpu as pltpu

F32 = jnp.float32
BF16 = jnp.bfloat16

HEAD_DIM = 64
GQA_GROUP = 8
ATTN_BLOCK = 128
TOP_K = 2
LN_EPS = 1e-5
NEG_INF = -1e30

V7X_LANES = 128
V7X_SUBLANES = 8
V7X_VMEM_BYTES = 64 * 1024 * 1024
V7X_VMEM_BUDGET = V7X_VMEM_BYTES - 8 * 1024 * 1024

MOE_TILE = 512


def _tile(dim, target, quantum=V7X_LANES):
    if dim <= target:
        return dim
    t = (target // quantum) * quantum
    while t > quantum and dim % t:
        t -= quantum
    assert dim % t == 0, (dim, target)
    return t


def _params(semantics, vmem_bytes):
    limit = min(V7X_VMEM_BUDGET, max(32 * 1024 * 1024, int(vmem_bytes)))
    return pltpu.CompilerParams(dimension_semantics=semantics, vmem_limit_bytes=limit)


def _dot(a, b):
    return jnp.dot(a, b, preferred_element_type=F32)


def _layer_norm(z, g, b):
    mu = jnp.mean(z, axis=-1, keepdims=True)
    zc = z - mu
    var = jnp.mean(zc * zc, axis=-1, keepdims=True)
    return zc * lax.rsqrt(var + LN_EPS) * g + b


def _matmul_kernel(x_ref, w_ref, o_ref, *, scale):
    acc = _dot(x_ref[...], w_ref[...])
    if scale != 1.0:
        acc = acc * scale
    o_ref[...] = acc.astype(o_ref.dtype)


def _matmul(x, w, *, out_dtype, scale=1.0, name):
    n, k = x.shape
    m = w.shape[1]
    tm = _tile(n, 1024)
    tn = _tile(m, 1024)
    vmem = 2 * (tm * k * 2 + k * tn * 2 + tm * tn * 4) + tm * tn * 4
    return pl.pallas_call(
        functools.partial(_matmul_kernel, scale=scale),
        grid=(m // tn, n // tm),
        in_specs=[pl.BlockSpec((tm, k), lambda j, i: (i, 0)),
                  pl.BlockSpec((k, tn), lambda j, i: (0, j))],
        out_specs=pl.BlockSpec((tm, tn), lambda j, i: (i, j)),
        out_shape=jax.ShapeDtypeStruct((n, m), out_dtype),
        compiler_params=_params(("parallel", "parallel"), vmem),
        name=name,
    )(x, w)


def _matmul_nt_kernel(wt_ref, x_ref, o_ref, *, scale):
    acc = lax.dot_general(wt_ref[...], x_ref[...], (((1,), (1,)), ((), ())),
                          preferred_element_type=F32)
    if scale != 1.0:
        acc = acc * scale
    o_ref[...] = acc.astype(o_ref.dtype)


def _matmul_nt(wt, x, *, out_dtype, scale=1.0, name):
    m, k = wt.shape
    n = x.shape[0]
    tm = _tile(n, 1024)
    tn = _tile(m, 1024)
    vmem = 2 * (tm * k * 2 + k * tn * 2 + tm * tn * 4) + tm * tn * 4
    return pl.pallas_call(
        functools.partial(_matmul_nt_kernel, scale=scale),
        grid=(m // tn, n // tm),
        in_specs=[pl.BlockSpec((tn, k), lambda j, i: (j, 0)),
                  pl.BlockSpec((tm, k), lambda j, i: (i, 0))],
        out_specs=pl.BlockSpec((tn, tm), lambda j, i: (j, i)),
        out_shape=jax.ShapeDtypeStruct((m, n), out_dtype),
        compiler_params=_params(("parallel", "parallel"), vmem),
        name=name,
    )(wt, x)


def _conv_gate_kernel(x_ref, wb_ref, wc_ref, wh_ref, cw_ref, o_ref, ubuf_ref, *, tiles_per_seq):
    i = pl.program_id(1)
    tm = x_ref.shape[0]
    pad = V7X_SUBLANES

    @pl.when(i % tiles_per_seq == 0)
    def _():
        ubuf_ref[0:pad, :] = jnp.zeros((pad, ubuf_ref.shape[1]), F32)

    x = x_ref[...]
    u = _dot(x, wc_ref[...]) * _dot(x, wh_ref[...])
    ubuf_ref[pad:pad + tm, :] = u
    u1 = ubuf_ref[pad - 1:pad - 1 + tm, :]
    u2 = ubuf_ref[pad - 2:pad - 2 + tm, :]
    cw = cw_ref[...]
    conv = cw[0:1, :] * u2 + cw[1:2, :] * u1 + cw[2:3, :] * u
    o_ref[...] = (_dot(x, wb_ref[...]) * conv).astype(o_ref.dtype)
    ubuf_ref[0:pad, :] = ubuf_ref[tm:tm + pad, :]


def _conv_gate(xb, w_in, conv_w, *, seq):
    n, d = xb.shape
    tm = _tile(seq, 1024)
    tn = _tile(d, 256)
    nj = d // tn
    vmem = 2 * (tm * d * 2 + 3 * d * tn * 2 + tm * tn * 2) + (tm + 8) * tn * 4 + 6 * tm * tn * 4
    return pl.pallas_call(
        functools.partial(_conv_gate_kernel, tiles_per_seq=seq // tm),
        grid=(nj, n // tm),
        in_specs=[pl.BlockSpec((tm, d), lambda j, i: (i, 0)),
                  pl.BlockSpec((d, tn), lambda j, i: (0, j)),
                  pl.BlockSpec((d, tn), lambda j, i: (0, nj + j)),
                  pl.BlockSpec((d, tn), lambda j, i: (0, 2 * nj + j)),
                  pl.BlockSpec((3, tn), lambda j, i: (0, j))],
        out_specs=pl.BlockSpec((tm, tn), lambda j, i: (i, j)),
        out_shape=jax.ShapeDtypeStruct((n, d), BF16),
        scratch_shapes=[pltpu.VMEM((tm + V7X_SUBLANES, tn), F32)],
        compiler_params=_params(("parallel", "arbitrary"), vmem),
        name="conv_gate",
    )(xb, w_in, w_in, w_in, conv_w)


def _proj_ln_kernel(a_ref, w_ref, res_ref, g_ref, b_ref, of_ref, ob_ref, acc_ref, *, alpha, nk):
    k = pl.program_id(1)
    part = _dot(a_ref[...], w_ref[...])

    def finish(acc):
        y = _layer_norm(alpha * res_ref[...] + acc, g_ref[...], b_ref[...])
        of_ref[...] = y
        ob_ref[...] = y.astype(BF16)

    if nk == 1:
        finish(part)
        return

    @pl.when(k == 0)
    def _():
        acc_ref[...] = part

    @pl.when(jnp.logical_and(k > 0, k < nk - 1))
    def _():
        acc_ref[...] += part

    @pl.when(k == nk - 1)
    def _():
        finish(acc_ref[...] + part)


def _proj_ln(a, w, res, g, b, *, alpha, name):
    n, kdim = a.shape
    d = w.shape[1]
    tm = _tile(n, 512)
    tk = kdim if kdim <= 2048 else _tile(kdim, 1536)
    nk = kdim // tk
    vmem = 2 * (tm * tk * 2 + tk * d * 2 + tm * d * 4 + tm * d * 6) + 3 * tm * d * 4
    return pl.pallas_call(
        functools.partial(_proj_ln_kernel, alpha=alpha, nk=nk),
        grid=(n // tm, nk),
        in_specs=[pl.BlockSpec((tm, tk), lambda i, k: (i, k)),
                  pl.BlockSpec((tk, d), lambda i, k: (k, 0)),
                  pl.BlockSpec((tm, d), lambda i, k: (i, 0)),
                  pl.BlockSpec((1, d), lambda i, k: (0, 0)),
                  pl.BlockSpec((1, d), lambda i, k: (0, 0))],
        out_specs=[pl.BlockSpec((tm, d), lambda i, k: (i, 0)),
                   pl.BlockSpec((tm, d), lambda i, k: (i, 0))],
        out_shape=[jax.ShapeDtypeStruct((n, d), F32), jax.ShapeDtypeStruct((n, d), BF16)],
        scratch_shapes=[pltpu.VMEM((tm, d), F32)],
        compiler_params=_params(("parallel", "arbitrary"), vmem),
        name=name,
    )(a, w, res, g, b)


def _swiglu_up_kernel(x_ref, wg_ref, wu_ref, o_ref):
    x = x_ref[...]
    o_ref[...] = (jax.nn.silu(_dot(x, wg_ref[...])) * _dot(x, wu_ref[...])).astype(o_ref.dtype)


def _swiglu_up(xb, wg, wu):
    n, d = xb.shape
    f = wg.shape[1]
    tm = _tile(n, 1024)
    tf = _tile(f, 512)
    vmem = 2 * (tm * d * 2 + 2 * d * tf * 2 + tm * tf * 2) + 4 * tm * tf * 4
    return pl.pallas_call(
        _swiglu_up_kernel,
        grid=(n // tm, f // tf),
        in_specs=[pl.BlockSpec((tm, d), lambda i, j: (i, 0)),
                  pl.BlockSpec((d, tf), lambda i, j: (0, j)),
                  pl.BlockSpec((d, tf), lambda i, j: (0, j))],
        out_specs=pl.BlockSpec((tm, tf), lambda i, j: (i, j)),
        out_shape=jax.ShapeDtypeStruct((n, f), BF16),
        compiler_params=_params(("parallel", "parallel"), vmem),
        name="swiglu_up",
    )(xb, wg, wu)


def _attn_kernel(sink_ref, qt_ref, kp_ref, kc_ref, vtp_ref, vtc_ref, o_ref, *, n_heads):
    has_prev = pl.program_id(1) > 0
    blk = o_ref.shape[0]
    key = lax.broadcasted_iota(jnp.int32, (blk, blk), 0)
    qry = lax.broadcasted_iota(jnp.int32, (blk, blk), 1)
    mask_c = key <= qry
    mask_p = jnp.logical_and(key > qry, has_prev)
    heads_per_store = V7X_LANES // HEAD_DIM
    done = []
    for h in range(n_heads):
        qs = slice(h * HEAD_DIM, (h + 1) * HEAD_DIM)
        ks = slice((h // GQA_GROUP) * HEAD_DIM, (h // GQA_GROUP + 1) * HEAD_DIM)
        qt = qt_ref[qs, :]
        s_c = jnp.where(mask_c, _dot(kc_ref[:, ks], qt), NEG_INF)
        s_p = jnp.where(mask_p, _dot(kp_ref[:, ks], qt), NEG_INF)
        sink = sink_ref[h]
        m = jnp.maximum(jnp.max(jnp.maximum(s_c, s_p), axis=0, keepdims=True), sink)
        p_c = jnp.exp(s_c - m)
        p_p = jnp.exp(s_p - m)
        denom = jnp.sum(p_c + p_p, axis=0, keepdims=True) + jnp.exp(sink - m)
        ot = _dot(vtc_ref[ks, :], p_c.astype(BF16)) + _dot(vtp_ref[ks, :], p_p.astype(BF16))
        done.append(ot / denom)
        if len(done) == heads_per_store:
            first = h + 1 - heads_per_store
            o_ref[:, first * HEAD_DIM:(h + 1) * HEAD_DIM] = (
                jnp.concatenate(done, axis=0).T.astype(o_ref.dtype))
            done = []


def _attention(qt, k, vt, sinks, *, bsz, seq):
    d, n = qt.shape
    kvd = k.shape[1]
    nb = seq // ATTN_BLOCK
    cur_row = lambda b, i: (b * nb + i, 0)
    prev_row = lambda b, i: (b * nb + jnp.maximum(i - 1, 0), 0)
    cur_col = lambda b, i: (0, b * nb + i)
    prev_col = lambda b, i: (0, b * nb + jnp.maximum(i - 1, 0))
    vmem = 2 * (2 * ATTN_BLOCK * d * 2 + 4 * ATTN_BLOCK * kvd * 2) + 64 * ATTN_BLOCK * ATTN_BLOCK * 4
    return pl.pallas_call(
        functools.partial(_attn_kernel, n_heads=d // HEAD_DIM),
        grid=(bsz, nb),
        in_specs=[pl.BlockSpec(memory_space=pltpu.SMEM),
                  pl.BlockSpec((d, ATTN_BLOCK), cur_col),
                  pl.BlockSpec((ATTN_BLOCK, kvd), prev_row),
                  pl.BlockSpec((ATTN_BLOCK, kvd), cur_row),
                  pl.BlockSpec((kvd, ATTN_BLOCK), prev_col),
                  pl.BlockSpec((kvd, ATTN_BLOCK), cur_col)],
        out_specs=pl.BlockSpec((ATTN_BLOCK, d), cur_row),
        out_shape=jax.ShapeDtypeStruct((n, d), BF16),
        compiler_params=_params(("parallel", "parallel"), vmem),
        name="swa_attention",
    )(sinks, qt, k, k, vt, vt)


def _router_kernel(x_ref, wr_ref, ri_ref, rg_ref, cnt_ref, carry_ref, *, n_experts):
    i = pl.program_id(0)
    tm = x_ref.shape[0]

    @pl.when(i == 0)
    def _():
        carry_ref[...] = jnp.zeros_like(carry_ref)

    logits = jnp.dot(x_ref[...], wr_ref[...], preferred_element_type=F32,
                     precision=lax.Precision.HIGHEST)
    lane = lax.broadcasted_iota(jnp.int32, logits.shape, 1)
    logits = jnp.where(lane < n_experts, logits, -jnp.inf)
    m1 = jnp.max(logits, axis=-1, keepdims=True)
    i1 = jnp.min(jnp.where(logits == m1, lane, V7X_LANES), axis=-1, keepdims=True)
    rest = jnp.where(lane == i1, -jnp.inf, logits)
    m2 = jnp.max(rest, axis=-1, keepdims=True)
    i2 = jnp.min(jnp.where(rest == m2, lane, V7X_LANES), axis=-1, keepdims=True)
    e2 = jnp.exp(m2 - m1)
    g1 = 1.0 / (1.0 + e2)
    g2 = e2 / (1.0 + e2)

    sel1 = lane == i1
    sel2 = lane == i2
    member = jnp.logical_or(sel1, sel2).astype(BF16)
    r = lax.broadcasted_iota(jnp.int32, (tm, tm), 0)
    c = lax.broadcasted_iota(jnp.int32, (tm, tm), 1)
    before = (c < r).astype(BF16)
    prior = carry_ref[...] + _dot(before, member)
    r1 = jnp.sum(jnp.where(sel1, prior, 0.0), axis=-1, keepdims=True).astype(jnp.int32)
    r2 = jnp.sum(jnp.where(sel2, prior, 0.0), axis=-1, keepdims=True).astype(jnp.int32)
    carry_ref[...] += jnp.sum(member.astype(F32), axis=0, keepdims=True)

    ri_ref[...] = jnp.where(lane == 0, i1, jnp.where(lane == 1, i2,
                            jnp.where(lane == 2, r1, jnp.where(lane == 3, r2, 0))))
    rg_ref[...] = jnp.where(lane == 0, g1, jnp.where(lane == 1, g2, 0.0))
    cnt_ref[...] = carry_ref[...].astype(jnp.int32)


def _router(x, w_router_padded, *, n_experts):
    n, d = x.shape
    tm = _tile(n, 512)
    vmem = 2 * (tm * d * 4 + d * V7X_LANES * 4 + 2 * tm * V7X_LANES * 4) + 8 * tm * tm * 4
    return pl.pallas_call(
        functools.partial(_router_kernel, n_experts=n_experts),
        grid=(n // tm,),
        in_specs=[pl.BlockSpec((tm, d), lambda i: (i, 0)),
                  pl.BlockSpec((d, V7X_LANES), lambda i: (0, 0))],
        out_specs=[pl.BlockSpec((tm, V7X_LANES), lambda i: (i, 0)),
                   pl.BlockSpec((tm, V7X_LANES), lambda i: (i, 0)),
                   pl.BlockSpec((1, V7X_LANES), lambda i: (0, 0))],
        out_shape=[jax.ShapeDtypeStruct((n, V7X_LANES), jnp.int32),
                   jax.ShapeDtypeStruct((n, V7X_LANES), F32),
                   jax.ShapeDtypeStruct((1, V7X_LANES), jnp.int32)],
        scratch_shapes=[pltpu.VMEM((1, V7X_LANES), F32)],
        compiler_params=_params(("arbitrary",), vmem),
        name="moe_router",
    )(x, w_router_padded)


def _dispatch_kernel(slot_ref, x_ref, init_hbm, xs_hbm, sem):
    del init_hbm
    chunk = x_ref.shape[0]

    def copies(r):
        return [pltpu.make_async_copy(x_ref.at[pl.ds(r, 1), :],
                                      xs_hbm.at[pl.ds(slot_ref[0, 0, TOP_K * r + k], 1), :], sem)
                for k in range(TOP_K)]

    def start(r, carry):
        for cp in copies(r):
            cp.start()
        return carry

    def wait(r, carry):
        for cp in copies(r):
            cp.wait()
        return carry

    lax.fori_loop(0, chunk, start, 0)
    lax.fori_loop(0, chunk, wait, 0)


def _dispatch(x, slots, *, n_slots):
    n, d = x.shape
    chunk = _tile(n, 512)
    slots3 = slots.reshape(n // chunk, 1, TOP_K * chunk)
    init = jnp.zeros((n_slots, d), x.dtype)
    return pl.pallas_call(
        _dispatch_kernel,
        grid=(n // chunk,),
        in_specs=[pl.BlockSpec((1, 1, TOP_K * chunk), lambda i: (i, 0, 0), memory_space=pltpu.SMEM),
                  pl.BlockSpec((chunk, d), lambda i: (i, 0)),
                  pl.BlockSpec(memory_space=pl.ANY)],
        out_specs=pl.BlockSpec(memory_space=pl.ANY),
        out_shape=jax.ShapeDtypeStruct((n_slots, d), x.dtype),
        scratch_shapes=[pltpu.SemaphoreType.DMA(())],
        input_output_aliases={2: 0},
        compiler_params=_params(("arbitrary",), 2 * chunk * d * 4),
        name="moe_dispatch",
    )(slots3, x, init)


def _cast_kernel(x_ref, o_ref):
    o_ref[...] = x_ref[...].astype(o_ref.dtype)


def _cast_bf16(x):
    n, d = x.shape
    tm = _tile(n, 1024)
    return pl.pallas_call(
        _cast_kernel,
        grid=(n // tm,),
        in_specs=[pl.BlockSpec((tm, d), lambda i: (i, 0))],
        out_specs=pl.BlockSpec((tm, d), lambda i: (i, 0)),
        out_shape=jax.ShapeDtypeStruct((n, d), BF16),
        compiler_params=_params(("parallel",), 2 * tm * d * 6),
        name="cast_bf16",
    )(x)


def _moe_up_kernel(te_ref, first_ref, nused_ref, xs_ref, wg_ref, wu_ref, o_ref, wgb_ref, wub_ref):
    del te_ref
    t = pl.program_id(1)

    @pl.when(first_ref[t] == 1)
    def _():
        wgb_ref[...] = wg_ref[...].astype(BF16)
        wub_ref[...] = wu_ref[...].astype(BF16)

    @pl.when(t < nused_ref[0])
    def _():
        x = xs_ref[...]
        o_ref[...] = (jax.nn.silu(_dot(x, wgb_ref[...])) * _dot(x, wub_ref[...])).astype(o_ref.dtype)

    @pl.when(t >= nused_ref[0])
    def _():
        o_ref[...] = jnp.zeros_like(o_ref)


def _moe_up(xs, w_gate, w_up, layer, tile_expert, tile_first, n_used):
    s, d = xs.shape
    f = w_gate.shape[3]
    tm = MOE_TILE
    tf = _tile(f, 1024)
    vmem = 2 * (tm * d * 2 + 2 * d * tf * 4 + tm * tf * 2) + 2 * d * tf * 2 + 3 * tm * tf * 4
    w_spec = pl.BlockSpec((None, None, d, tf), lambda j, t, te, fi, nu: (layer, te[t], 0, j))
    grid_spec = pltpu.PrefetchScalarGridSpec(
        num_scalar_prefetch=3,
        grid=(f // tf, s // tm),
        in_specs=[pl.BlockSpec((tm, d), lambda j, t, te, fi, nu: (t, 0)), w_spec, w_spec],
        out_specs=pl.BlockSpec((tm, tf), lambda j, t, te, fi, nu: (t, j)),
        scratch_shapes=[pltpu.VMEM((d, tf), BF16), pltpu.VMEM((d, tf), BF16)],
    )
    return pl.pallas_call(
        _moe_up_kernel,
        grid_spec=grid_spec,
        out_shape=jax.ShapeDtypeStruct((s, f), BF16),
        compiler_params=_params(("arbitrary", "arbitrary"), vmem),
        name="moe_up",
    )(tile_expert, tile_first, n_used, xs, w_gate, w_up)


def _moe_down_kernel(te_ref, first_ref, nused_ref, h_ref, wd_ref, o_ref, wdb_ref):
    del te_ref
    t = pl.program_id(1)

    @pl.when(first_ref[t] == 1)
    def _():
        wdb_ref[...] = wd_ref[...].astype(BF16)

    @pl.when(t < nused_ref[0])
    def _():
        o_ref[...] = _dot(h_ref[...], wdb_ref[...])

    @pl.when(t >= nused_ref[0])
    def _():
        o_ref[...] = jnp.zeros_like(o_ref)


def _moe_down(h, w_down, layer, tile_expert, tile_first, n_used):
    s, f = h.shape
    d = w_down.shape[3]
    tm = MOE_TILE
    tn = _tile(d, 512)
    vmem = 2 * (tm * f * 2 + tm * tn * 4) + f * tn * 4 + f * tn * 2 + 2 * tm * tn * 4
    grid_spec = pltpu.PrefetchScalarGridSpec(
        num_scalar_prefetch=3,
        grid=(d // tn, s // tm),
        in_specs=[pl.BlockSpec((tm, f), lambda j, t, te, fi, nu: (t, 0)),
                  pl.BlockSpec((None, None, f, tn), lambda j, t, te, fi, nu: (layer, te[t], 0, j),
                               pipeline_mode=pl.Buffered(1))],
        out_specs=pl.BlockSpec((tm, tn), lambda j, t, te, fi, nu: (t, j)),
        scratch_shapes=[pltpu.VMEM((f, tn), BF16)],
    )
    return pl.pallas_call(
        _moe_down_kernel,
        grid_spec=grid_spec,
        out_shape=jax.ShapeDtypeStruct((s, d), F32),
        compiler_params=_params(("arbitrary", "arbitrary"), vmem),
        name="moe_down",
    )(tile_expert, tile_first, n_used, h, w_down)


def _combine_ln_kernel(slot_ref, y_hbm, rg_ref, res_ref, g_ref, b_ref, of_ref, ob_ref, buf_ref, sem,
                       *, alpha):
    tm = res_ref.shape[0]

    def copies(r):
        return [pltpu.make_async_copy(y_hbm.at[pl.ds(slot_ref[0, 0, TOP_K * r + k], 1), :],
                                      buf_ref.at[k, pl.ds(r, 1), :], sem) for k in range(TOP_K)]

    def start(r, carry):
        for cp in copies(r):
            cp.start()
        return carry

    def wait(r, carry):
        for cp in copies(r):
            cp.wait()
        return carry

    lax.fori_loop(0, tm, start, 0)
    lax.fori_loop(0, tm, wait, 0)
    gates = rg_ref[...]
    ff = gates[:, 0:1] * buf_ref[0] + gates[:, 1:2] * buf_ref[1]
    y = _layer_norm(alpha * res_ref[...] + ff, g_ref[...], b_ref[...])
    of_ref[...] = y
    ob_ref[...] = y.astype(BF16)


def _combine_ln(y, slots, gates, res, g, b, *, alpha):
    n, d = res.shape
    tm = _tile(n, 256)
    slots3 = slots.reshape(n // tm, 1, TOP_K * tm)
    vmem = 2 * (tm * V7X_LANES * 4 + tm * d * 4 + tm * d * 6) + TOP_K * tm * d * 4 + 3 * tm * d * 4
    return pl.pallas_call(
        functools.partial(_combine_ln_kernel, alpha=alpha),
        grid=(n // tm,),
        in_specs=[pl.BlockSpec((1, 1, TOP_K * tm), lambda i: (i, 0, 0), memory_space=pltpu.SMEM),
                  pl.BlockSpec(memory_space=pl.ANY),
                  pl.BlockSpec((tm, V7X_LANES), lambda i: (i, 0)),
                  pl.BlockSpec((tm, d), lambda i: (i, 0)),
                  pl.BlockSpec((1, d), lambda i: (0, 0)),
                  pl.BlockSpec((1, d), lambda i: (0, 0))],
        out_specs=[pl.BlockSpec((tm, d), lambda i: (i, 0)),
                   pl.BlockSpec((tm, d), lambda i: (i, 0))],
        out_shape=[jax.ShapeDtypeStruct((n, d), F32), jax.ShapeDtypeStruct((n, d), BF16)],
        scratch_shapes=[pltpu.VMEM((TOP_K, tm, d), F32), pltpu.SemaphoreType.DMA(())],
        compiler_params=_params(("arbitrary",), vmem),
        name="moe_combine_ln",
    )(slots3, y, gates, res, g, b)


def _moe_layer(x, w_router, w_gate, w_up, w_down, layer, g, b, *, alpha):
    n, d = x.shape
    n_experts = w_router.shape[1]
    tm = MOE_TILE
    n_tiles = (TOP_K * n) // tm + n_experts
    wr = jnp.pad(w_router, ((0, 0), (0, V7X_LANES - n_experts)))
    ri, rg, cnt = _router(x, wr, n_experts=n_experts)

    counts = cnt[0, :n_experts]
    tiles_per = (counts + tm - 1) // tm
    tile_end = jnp.cumsum(tiles_per)
    n_used = tile_end[-1]
    group_start = (tile_end - tiles_per) * tm
    tile_id = jnp.arange(n_tiles, dtype=jnp.int32)
    last_used = jnp.minimum(tile_id, jnp.maximum(n_used - 1, 0))
    tile_expert = jnp.sum((last_used[:, None] >= tile_end[None, :]).astype(jnp.int32), axis=1)
    tile_expert = jnp.minimum(tile_expert, n_experts - 1)
    tile_first = jnp.concatenate([jnp.ones((1,), jnp.int32),
                                  (tile_expert[1:] != tile_expert[:-1]).astype(jnp.int32)])
    slots = (group_start[ri[:, :TOP_K]] + ri[:, TOP_K:2 * TOP_K]).astype(jnp.int32).reshape(-1)
    n_used = n_used.astype(jnp.int32).reshape(1)

    xs = _cast_bf16(_dispatch(x, slots, n_slots=n_tiles * tm))
    h = _moe_up(xs, w_gate, w_up, layer, tile_expert, tile_first, n_used)
    y = _moe_down(h, w_down, layer, tile_expert, tile_first, n_used)
    return _combine_ln(y, slots, rg, x, g, b, alpha=alpha)


def kernel(x, conv_w_in, conv_w, conv_w_out, w_kv, attn_w_q, attn_sinks, attn_w_o, ln_g, ln_b,
           ffn_w_gate, ffn_w_up, ffn_w_down, moe_w_router, moe_w_gate, moe_w_up, moe_w_down):
    bsz, seq, d = x.shape
    n = bsz * seq
    depth = ln_g.shape[0]
    n_a = conv_w_in.shape[0]
    alpha = float((2 * depth) ** 0.25)
    assert seq % ATTN_BLOCK == 0 and d % (HEAD_DIM * GQA_GROUP) == 0
    assert w_kv.shape[1] == 2 * d // GQA_GROUP

    xf = x.reshape(n, d)
    xb = xf.astype(BF16)
    k_shared = vt_shared = None
    for l in range(depth):
        g0, b0 = ln_g[l, 0].reshape(1, d), ln_b[l, 0].reshape(1, d)
        g1, b1 = ln_g[l, 1].reshape(1, d), ln_b[l, 1].reshape(1, d)
        if l < n_a:
            gated = _conv_gate(xb, conv_w_in[l].astype(BF16), conv_w[l], seq=seq)
            xf, xb = _proj_ln(gated, conv_w_out[l].astype(BF16), xf, g0, b0, alpha=alpha,
                              name="conv_out_ln")
        else:
            j = l - n_a
            qt = _matmul_nt(attn_w_q[j].T.astype(BF16), xb, out_dtype=BF16,
                            scale=1.0 / math.sqrt(HEAD_DIM), name="attn_q")
            o = _attention(qt, k_shared, vt_shared, attn_sinks[j], bsz=bsz, seq=seq)
            xf, xb = _proj_ln(o, attn_w_o[j].astype(BF16), xf, g0, b0, alpha=alpha, name="attn_out_ln")
        i = l // 2
        if l % 2 == 0:
            h = _swiglu_up(xb, ffn_w_gate[i].astype(BF16), ffn_w_up[i].astype(BF16))
            xf, xb = _proj_ln(h, ffn_w_down[i].astype(BF16), xf, g1, b1, alpha=alpha, name="ffn_down_ln")
        else:
            xf, xb = _moe_layer(xf, moe_w_router[i], moe_w_gate, moe_w_up, moe_w_down, i,
                                g1, b1, alpha=alpha)
        if l == n_a - 1:
            kvd = w_kv.shape[1] // 2
            k_shared = _matmul(xb, w_kv[:, :kvd].astype(BF16), out_dtype=BF16, name="k_proj")
            vt_shared = _matmul_nt(w_kv[:, kvd:].T.astype(BF16), xb, out_dtype=BF16, name="v_proj")
    return xf.reshape(bsz, seq, d)
```

```python
import functools
import math

import jax
import jax.numpy as jnp
from jax import lax
from jax.experimental import pallas as pl
from jax.experimental.pallas import tpu as pltpu

F32 = jnp.float32
BF16 = jnp.bfloat16

HEAD_DIM = 64
GQA_GROUP = 8
ATTN_BLOCK = 128
TOP_K = 2
LN_EPS = 1e-5
NEG_INF = -1e30

V7X_LANES = 128
V7X_SUBLANES = 8
V7X_VMEM_BYTES = 64 * 1024 * 1024
V7X_VMEM_BUDGET = V7X_VMEM_BYTES - 8 * 1024 * 1024

MOE_TILE = 512


def _tile(dim, target, quantum=V7X_LANES):
    if dim <= target:
        return dim
    t = (target // quantum) * quantum
    while t > quantum and dim % t:
        t -= quantum
    assert dim % t == 0, (dim, target)
    return t


def _params(semantics, vmem_bytes):
    limit = min(V7X_VMEM_BUDGET, max(32 * 1024 * 1024, int(vmem_bytes)))
    return pltpu.CompilerParams(dimension_semantics=semantics, vmem_limit_bytes=limit)


def _dot(a, b):
    return jnp.dot(a, b, preferred_element_type=F32)


def _layer_norm(z, g, b):
    mu = jnp.mean(z, axis=-1, keepdims=True)
    zc = z - mu
    var = jnp.mean(zc * zc, axis=-1, keepdims=True)
    return zc * lax.rsqrt(var + LN_EPS) * g + b


def _matmul_kernel(x_ref, w_ref, o_ref, *, scale):
    acc = _dot(x_ref[...], w_ref[...])
    if scale != 1.0:
        acc = acc * scale
    o_ref[...] = acc.astype(o_ref.dtype)


def _matmul(x, w, *, out_dtype, scale=1.0, name):
    n, k = x.shape
    m = w.shape[1]
    tm = _tile(n, 1024)
    tn = _tile(m, 1024)
    vmem = 2 * (tm * k * 2 + k * tn * 2 + tm * tn * 4) + tm * tn * 4
    return pl.pallas_call(
        functools.partial(_matmul_kernel, scale=scale),
        grid=(m // tn, n // tm),
        in_specs=[pl.BlockSpec((tm, k), lambda j, i: (i, 0)),
                  pl.BlockSpec((k, tn), lambda j, i: (0, j))],
        out_specs=pl.BlockSpec((tm, tn), lambda j, i: (i, j)),
        out_shape=jax.ShapeDtypeStruct((n, m), out_dtype),
        compiler_params=_params(("parallel", "parallel"), vmem),
        name=name,
    )(x, w)


def _matmul_nt_kernel(wt_ref, x_ref, o_ref, *, scale):
    acc = lax.dot_general(wt_ref[...], x_ref[...], (((1,), (1,)), ((), ())),
                          preferred_element_type=F32)
    if scale != 1.0:
        acc = acc * scale
    o_ref[...] = acc.astype(o_ref.dtype)


def _matmul_nt(wt, x, *, out_dtype, scale=1.0, name):
    m, k = wt.shape
    n = x.shape[0]
    tm = _tile(n, 1024)
    tn = _tile(m, 1024)
    vmem = 2 * (tm * k * 2 + k * tn * 2 + tm * tn * 4) + tm * tn * 4
    return pl.pallas_call(
        functools.partial(_matmul_nt_kernel, scale=scale),
        grid=(m // tn, n // tm),
        in_specs=[pl.BlockSpec((tn, k), lambda j, i: (j, 0)),
                  pl.BlockSpec((tm, k), lambda j, i: (i, 0))],
        out_specs=pl.BlockSpec((tn, tm), lambda j, i: (j, i)),
        out_shape=jax.ShapeDtypeStruct((m, n), out_dtype),
        compiler_params=_params(("parallel", "parallel"), vmem),
        name=name,
    )(wt, x)


def _conv_gate_kernel(x_ref, wb_ref, wc_ref, wh_ref, cw_ref, o_ref, w16_ref, ubuf_ref, *, tiles_per_seq):
    i = pl.program_id(1)
    tm = x_ref.shape[0]
    pad = V7X_SUBLANES

    @pl.when(i == 0)
    def _():
        w16_ref[0] = wb_ref[...].astype(BF16)
        w16_ref[1] = wc_ref[...].astype(BF16)
        w16_ref[2] = wh_ref[...].astype(BF16)

    @pl.when(i % tiles_per_seq == 0)
    def _():
        ubuf_ref[0:pad, :] = jnp.zeros((pad, ubuf_ref.shape[1]), F32)

    x = x_ref[...]
    u = _dot(x, w16_ref[1]) * _dot(x, w16_ref[2])
    ubuf_ref[pad:pad + tm, :] = u
    u1 = ubuf_ref[pad - 1:pad - 1 + tm, :]
    u2 = ubuf_ref[pad - 2:pad - 2 + tm, :]
    cw = cw_ref[...]
    conv = cw[0:1, :] * u2 + cw[1:2, :] * u1 + cw[2:3, :] * u
    o_ref[...] = (_dot(x, w16_ref[0]) * conv).astype(o_ref.dtype)
    ubuf_ref[0:pad, :] = ubuf_ref[tm:tm + pad, :]


def _conv_gate(xb, w_in, conv_w, layer, *, seq):
    n, d = xb.shape
    tm = _tile(seq, 1024)
    tn = _tile(d, 256)
    nj = d // tn
    vmem = (2 * (tm * d * 2 + 3 * d * tn * 4 + tm * tn * 2) + 3 * d * tn * 2
            + (tm + 8) * tn * 4 + 6 * tm * tn * 4)
    w_spec = lambda third: pl.BlockSpec((None, d, tn), lambda j, i: (layer, 0, third * nj + j))
    return pl.pallas_call(
        functools.partial(_conv_gate_kernel, tiles_per_seq=seq // tm),
        grid=(nj, n // tm),
        in_specs=[pl.BlockSpec((tm, d), lambda j, i: (i, 0)),
                  w_spec(0), w_spec(1), w_spec(2),
                  pl.BlockSpec((None, 3, tn), lambda j, i: (layer, 0, j))],
        out_specs=pl.BlockSpec((tm, tn), lambda j, i: (i, j)),
        out_shape=jax.ShapeDtypeStruct((n, d), BF16),
        scratch_shapes=[pltpu.VMEM((3, d, tn), BF16), pltpu.VMEM((tm + V7X_SUBLANES, tn), F32)],
        compiler_params=_params(("arbitrary", "arbitrary"), vmem),
        name="conv_gate",
    )(xb, w_in, w_in, w_in, conv_w)


def _pack_bf16_pairs(y):
    half = y.shape[1] // 2
    lo = lax.bitcast_convert_type(y[:, :half].astype(BF16).astype(F32), jnp.uint32)
    hi = lax.bitcast_convert_type(y[:, half:].astype(BF16).astype(F32), jnp.uint32)
    return (lo >> 16) | (hi & jnp.uint32(0xFFFF0000))


def _unpack_bf16_pairs(p):
    lo = lax.bitcast_convert_type(p << 16, F32).astype(BF16)
    hi = lax.bitcast_convert_type(p & jnp.uint32(0xFFFF0000), F32).astype(BF16)
    return lo, hi


def _proj_ln_kernel(a_ref, w_ref, res_ref, g_ref, b_ref, of_ref, ob_ref, *, alpha, nk, row_chunks, packed):
    k = pl.program_id(1)
    rc = a_ref.shape[0] // row_chunks

    def finish(rows, acc):
        y = _layer_norm(alpha * res_ref[rows, :] + acc, g_ref[...], b_ref[...])
        of_ref[rows, :] = y
        ob_ref[rows, :] = _pack_bf16_pairs(y) if packed else y.astype(BF16)

    for c in range(row_chunks):
        rows = pl.ds(c * rc, rc)
        part = _dot(a_ref[rows, :], w_ref[...])
        if nk == 1:
            finish(rows, part)
            continue

        @pl.when(k == 0)
        def _():
            of_ref[rows, :] = part

        @pl.when(jnp.logical_and(k > 0, k < nk - 1))
        def _():
            of_ref[rows, :] += part

        @pl.when(k == nk - 1)
        def _():
            finish(rows, of_ref[rows, :] + part)


def _proj_ln(a, w, res, g, b, *, alpha, name, packed=False):
    n, kdim = a.shape
    d = w.shape[1]
    split = kdim > 2048
    tm = _tile(n, 1024 if split else 512)
    tk = _tile(kdim, 768) if split else kdim
    nk = kdim // tk
    row_chunks = 4 if split else 2
    vmem = 2 * (tm * tk * 2 + tk * d * 2 + tm * d * 4 + tm * d * 6) + 3 * (tm // row_chunks) * d * 4
    ob_shape = jax.ShapeDtypeStruct((n, d // 2), jnp.uint32) if packed else jax.ShapeDtypeStruct((n, d), BF16)
    return pl.pallas_call(
        functools.partial(_proj_ln_kernel, alpha=alpha, nk=nk, row_chunks=row_chunks, packed=packed),
        grid=(n // tm, nk),
        in_specs=[pl.BlockSpec((tm, tk), lambda i, k: (i, k)),
                  pl.BlockSpec((tk, d), lambda i, k: (k, 0)),
                  pl.BlockSpec((tm, d), lambda i, k: (i, 0)),
                  pl.BlockSpec((1, d), lambda i, k: (0, 0)),
                  pl.BlockSpec((1, d), lambda i, k: (0, 0))],
        out_specs=[pl.BlockSpec((tm, d), lambda i, k: (i, 0)),
                   pl.BlockSpec((tm, ob_shape.shape[1]), lambda i, k: (i, 0))],
        out_shape=[jax.ShapeDtypeStruct((n, d), F32), ob_shape],
        compiler_params=_params(("parallel", "arbitrary"), vmem),
        name=name,
    )(a, w, res, g, b)


def _swiglu_up_kernel(x_ref, wg_ref, wu_ref, o_ref, wg16_ref, wu16_ref):
    @pl.when(pl.program_id(1) == 0)
    def _():
        wg16_ref[...] = wg_ref[...].astype(BF16)
        wu16_ref[...] = wu_ref[...].astype(BF16)

    x = x_ref[...]
    o_ref[...] = (jax.nn.silu(_dot(x, wg16_ref[...])) * _dot(x, wu16_ref[...])).astype(o_ref.dtype)


def _swiglu_up(xb, wg, wu, layer):
    n, d = xb.shape
    f = wg.shape[2]
    tm = _tile(n, 1024)
    tf = _tile(f, 512)
    vmem = 2 * (tm * d * 2 + 2 * d * tf * 4 + tm * tf * 2) + 2 * d * tf * 2 + 4 * tm * tf * 4
    w_spec = pl.BlockSpec((None, d, tf), lambda j, i: (layer, 0, j))
    return pl.pallas_call(
        _swiglu_up_kernel,
        grid=(f // tf, n // tm),
        in_specs=[pl.BlockSpec((tm, d), lambda j, i: (i, 0)), w_spec, w_spec],
        out_specs=pl.BlockSpec((tm, tf), lambda j, i: (i, j)),
        out_shape=jax.ShapeDtypeStruct((n, f), BF16),
        scratch_shapes=[pltpu.VMEM((d, tf), BF16), pltpu.VMEM((d, tf), BF16)],
        compiler_params=_params(("arbitrary", "arbitrary"), vmem),
        name="swiglu_up",
    )(xb, wg, wu)


def _attn_kernel(sink_ref, qt_ref, kp_ref, kc_ref, vtp_ref, vtc_ref, o_ref, *, n_heads):
    has_prev = pl.program_id(1) > 0
    blk = o_ref.shape[0]
    key = lax.broadcasted_iota(jnp.int32, (blk, blk), 0)
    qry = lax.broadcasted_iota(jnp.int32, (blk, blk), 1)
    mask_c = key <= qry
    mask_p = jnp.logical_and(key > qry, has_prev)
    heads_per_store = V7X_LANES // HEAD_DIM
    done = []
    for h in range(n_heads):
        qs = slice(h * HEAD_DIM, (h + 1) * HEAD_DIM)
        ks = slice((h // GQA_GROUP) * HEAD_DIM, (h // GQA_GROUP + 1) * HEAD_DIM)
        qt = qt_ref[qs, :]
        s_c = jnp.where(mask_c, _dot(kc_ref[:, ks], qt), NEG_INF)
        s_p = jnp.where(mask_p, _dot(kp_ref[:, ks], qt), NEG_INF)
        sink = sink_ref[h]
        m = jnp.maximum(jnp.max(jnp.maximum(s_c, s_p), axis=0, keepdims=True), sink)
        p_c = jnp.exp(s_c - m)
        p_p = jnp.exp(s_p - m)
        denom = jnp.sum(p_c + p_p, axis=0, keepdims=True) + jnp.exp(sink - m)
        ot = _dot(vtc_ref[ks, :], p_c.astype(BF16)) + _dot(vtp_ref[ks, :], p_p.astype(BF16))
        done.append(ot / denom)
        if len(done) == heads_per_store:
            first = h + 1 - heads_per_store
            o_ref[:, first * HEAD_DIM:(h + 1) * HEAD_DIM] = (
                jnp.concatenate(done, axis=0).T.astype(o_ref.dtype))
            done = []


def _attention(qt, k, vt, sinks, *, bsz, seq):
    d, n = qt.shape
    kvd = k.shape[1]
    nb = seq // ATTN_BLOCK
    cur_row = lambda b, i: (b * nb + i, 0)
    prev_row = lambda b, i: (b * nb + jnp.maximum(i - 1, 0), 0)
    cur_col = lambda b, i: (0, b * nb + i)
    prev_col = lambda b, i: (0, b * nb + jnp.maximum(i - 1, 0))
    vmem = 2 * (2 * ATTN_BLOCK * d * 2 + 4 * ATTN_BLOCK * kvd * 2) + 64 * ATTN_BLOCK * ATTN_BLOCK * 4
    return pl.pallas_call(
        functools.partial(_attn_kernel, n_heads=d // HEAD_DIM),
        grid=(bsz, nb),
        in_specs=[pl.BlockSpec(memory_space=pltpu.SMEM),
                  pl.BlockSpec((d, ATTN_BLOCK), cur_col),
                  pl.BlockSpec((ATTN_BLOCK, kvd), prev_row),
                  pl.BlockSpec((ATTN_BLOCK, kvd), cur_row),
                  pl.BlockSpec((kvd, ATTN_BLOCK), prev_col),
                  pl.BlockSpec((kvd, ATTN_BLOCK), cur_col)],
        out_specs=pl.BlockSpec((ATTN_BLOCK, d), cur_row),
        out_shape=jax.ShapeDtypeStruct((n, d), BF16),
        compiler_params=_params(("parallel", "parallel"), vmem),
        name="swa_attention",
    )(sinks, qt, k, k, vt, vt)


def _router_kernel(x_ref, wr_ref, ri_ref, rg_ref, cnt_ref, carry_ref, *, n_experts):
    i = pl.program_id(0)
    tm = x_ref.shape[0]

    @pl.when(i == 0)
    def _():
        carry_ref[...] = jnp.zeros_like(carry_ref)

    logits = jnp.dot(x_ref[...], wr_ref[...], preferred_element_type=F32,
                     precision=lax.Precision.HIGHEST)
    lane = lax.broadcasted_iota(jnp.int32, logits.shape, 1)
    logits = jnp.where(lane < n_experts, logits, -jnp.inf)
    m1 = jnp.max(logits, axis=-1, keepdims=True)
    i1 = jnp.min(jnp.where(logits == m1, lane, V7X_LANES), axis=-1, keepdims=True)
    rest = jnp.where(lane == i1, -jnp.inf, logits)
    m2 = jnp.max(rest, axis=-1, keepdims=True)
    i2 = jnp.min(jnp.where(rest == m2, lane, V7X_LANES), axis=-1, keepdims=True)
    e2 = jnp.exp(m2 - m1)
    g1 = 1.0 / (1.0 + e2)
    g2 = e2 / (1.0 + e2)

    sel1 = lane == i1
    sel2 = lane == i2
    member = jnp.logical_or(sel1, sel2).astype(BF16)
    r = lax.broadcasted_iota(jnp.int32, (tm, tm), 0)
    c = lax.broadcasted_iota(jnp.int32, (tm, tm), 1)
    before = (c < r).astype(BF16)
    prior = carry_ref[...] + _dot(before, member)
    r1 = jnp.sum(jnp.where(sel1, prior, 0.0), axis=-1, keepdims=True).astype(jnp.int32)
    r2 = jnp.sum(jnp.where(sel2, prior, 0.0), axis=-1, keepdims=True).astype(jnp.int32)
    carry_ref[...] += jnp.sum(member.astype(F32), axis=0, keepdims=True)

    ri_ref[...] = jnp.where(lane == 0, i1, jnp.where(lane == 1, i2,
                            jnp.where(lane == 2, r1, jnp.where(lane == 3, r2, 0))))
    rg_ref[...] = jnp.where(lane == 0, g1, jnp.where(lane == 1, g2, 0.0))
    cnt_ref[...] = carry_ref[...].astype(jnp.int32)


def _router(x, w_router_padded, *, n_experts):
    n, d = x.shape
    tm = _tile(n, 512)
    vmem = 2 * (tm * d * 4 + d * V7X_LANES * 4 + 2 * tm * V7X_LANES * 4) + 8 * tm * tm * 4
    return pl.pallas_call(
        functools.partial(_router_kernel, n_experts=n_experts),
        grid=(n // tm,),
        in_specs=[pl.BlockSpec((tm, d), lambda i: (i, 0)),
                  pl.BlockSpec((d, V7X_LANES), lambda i: (0, 0))],
        out_specs=[pl.BlockSpec((tm, V7X_LANES), lambda i: (i, 0)),
                   pl.BlockSpec((tm, V7X_LANES), lambda i: (i, 0)),
                   pl.BlockSpec((1, V7X_LANES), lambda i: (0, 0))],
        out_shape=[jax.ShapeDtypeStruct((n, V7X_LANES), jnp.int32),
                   jax.ShapeDtypeStruct((n, V7X_LANES), F32),
                   jax.ShapeDtypeStruct((1, V7X_LANES), jnp.int32)],
        scratch_shapes=[pltpu.VMEM((1, V7X_LANES), F32)],
        compiler_params=_params(("arbitrary",), vmem),
        name="moe_router",
    )(x, w_router_padded)


def _dispatch_kernel(slot_ref, x_ref, init_hbm, xs_hbm, sem):
    del init_hbm
    chunk = x_ref.shape[0]

    def copies(r):
        return [pltpu.make_async_copy(x_ref.at[pl.ds(r, 1), :],
                                      xs_hbm.at[pl.ds(slot_ref[0, 0, TOP_K * r + k], 1), :], sem)
                for k in range(TOP_K)]

    def start(r, carry):
        for cp in copies(r):
            cp.start()
        return carry

    lax.fori_loop(0, chunk, start, 0)
    for _ in range(TOP_K):
        pltpu.make_async_copy(x_ref, xs_hbm.at[pl.ds(0, chunk), :], sem).wait()


def _dispatch(x, slots, *, n_slots):
    n, d = x.shape
    chunk = _tile(n, 512)
    slots3 = slots.reshape(n // chunk, 1, TOP_K * chunk)
    init = jnp.zeros((n_slots, d), x.dtype)
    return pl.pallas_call(
        _dispatch_kernel,
        grid=(n // chunk,),
        in_specs=[pl.BlockSpec((1, 1, TOP_K * chunk), lambda i: (i, 0, 0), memory_space=pltpu.SMEM),
                  pl.BlockSpec((chunk, d), lambda i: (i, 0)),
                  pl.BlockSpec(memory_space=pl.ANY)],
        out_specs=pl.BlockSpec(memory_space=pl.ANY),
        out_shape=jax.ShapeDtypeStruct((n_slots, d), x.dtype),
        scratch_shapes=[pltpu.SemaphoreType.DMA(())],
        input_output_aliases={2: 0},
        compiler_params=_params(("arbitrary",), 2 * chunk * d * 4),
        name="moe_dispatch",
    )(slots3, x, init)


def _moe_up_kernel(te_ref, first_ref, nused_ref, xs_ref, wg_ref, wu_ref, o_ref, wgb_ref, wub_ref):
    del te_ref
    t = pl.program_id(1)
    half = xs_ref.shape[1]

    @pl.when(first_ref[t] == 1)
    def _():
        wgb_ref[...] = wg_ref[...].astype(BF16)
        wub_ref[...] = wu_ref[...].astype(BF16)

    @pl.when(t < nused_ref[0])
    def _():
        lo, hi = _unpack_bf16_pairs(xs_ref[...])
        gate = _dot(lo, wgb_ref[0:half, :]) + _dot(hi, wgb_ref[half:2 * half, :])
        up = _dot(lo, wub_ref[0:half, :]) + _dot(hi, wub_ref[half:2 * half, :])
        o_ref[...] = (jax.nn.silu(gate) * up).astype(o_ref.dtype)

    @pl.when(t >= nused_ref[0])
    def _():
        o_ref[...] = jnp.zeros_like(o_ref)


def _moe_up(xs, w_gate, w_up, layer, tile_expert, tile_first, n_used):
    s = xs.shape[0]
    d = 2 * xs.shape[1]
    f = w_gate.shape[3]
    tm = MOE_TILE
    tf = _tile(f, 1024)
    vmem = 2 * (tm * d * 2 + 2 * d * tf * 4 + tm * tf * 2) + 2 * d * tf * 2 + 3 * tm * tf * 4
    w_spec = pl.BlockSpec((None, None, d, tf), lambda j, t, te, fi, nu: (layer, te[t], 0, j))
    grid_spec = pltpu.PrefetchScalarGridSpec(
        num_scalar_prefetch=3,
        grid=(f // tf, s // tm),
        in_specs=[pl.BlockSpec((tm, d // 2), lambda j, t, te, fi, nu: (t, 0)), w_spec, w_spec],
        out_specs=pl.BlockSpec((tm, tf), lambda j, t, te, fi, nu: (t, j)),
        scratch_shapes=[pltpu.VMEM((d, tf), BF16), pltpu.VMEM((d, tf), BF16)],
    )
    return pl.pallas_call(
        _moe_up_kernel,
        grid_spec=grid_spec,
        out_shape=jax.ShapeDtypeStruct((s, f), BF16),
        compiler_params=_params(("arbitrary", "arbitrary"), vmem),
        name="moe_up",
    )(tile_expert, tile_first, n_used, xs, w_gate, w_up)


def _moe_down_kernel(te_ref, first_ref, nused_ref, h_ref, wd_ref, o_ref, wdb_ref):
    del te_ref
    t = pl.program_id(1)

    @pl.when(first_ref[t] == 1)
    def _():
        wdb_ref[...] = wd_ref[...].astype(BF16)

    @pl.when(t < nused_ref[0])
    def _():
        o_ref[...] = _dot(h_ref[...], wdb_ref[...])

    @pl.when(t >= nused_ref[0])
    def _():
        o_ref[...] = jnp.zeros_like(o_ref)


def _moe_down(h, w_down, layer, tile_expert, tile_first, n_used):
    s, f = h.shape
    d = w_down.shape[3]
    tm = MOE_TILE
    tn = _tile(d, 512)
    vmem = 2 * (tm * f * 2 + tm * tn * 4 + f * tn * 4) + f * tn * 2 + 2 * tm * tn * 4
    grid_spec = pltpu.PrefetchScalarGridSpec(
        num_scalar_prefetch=3,
        grid=(d // tn, s // tm),
        in_specs=[pl.BlockSpec((tm, f), lambda j, t, te, fi, nu: (t, 0)),
                  pl.BlockSpec((None, None, f, tn), lambda j, t, te, fi, nu: (layer, te[t], 0, j))],
        out_specs=pl.BlockSpec((tm, tn), lambda j, t, te, fi, nu: (t, j)),
        scratch_shapes=[pltpu.VMEM((f, tn), BF16)],
    )
    return pl.pallas_call(
        _moe_down_kernel,
        grid_spec=grid_spec,
        out_shape=jax.ShapeDtypeStruct((s, d), F32),
        compiler_params=_params(("arbitrary", "arbitrary"), vmem),
        name="moe_down",
    )(tile_expert, tile_first, n_used, h, w_down)


def _combine_ln_kernel(slot_ref, y_hbm, rg_ref, res_ref, g_ref, b_ref, of_ref, ob_ref, buf_ref, sem,
                       *, alpha):
    tm = res_ref.shape[0]

    def copies(r):
        return [pltpu.make_async_copy(y_hbm.at[pl.ds(slot_ref[0, 0, TOP_K * r + k], 1), :],
                                      buf_ref.at[k, pl.ds(r, 1), :], sem) for k in range(TOP_K)]

    def start(r, carry):
        for cp in copies(r):
            cp.start()
        return carry

    lax.fori_loop(0, tm, start, 0)
    for k in range(TOP_K):
        pltpu.make_async_copy(y_hbm.at[pl.ds(0, tm), :], buf_ref.at[k], sem).wait()
    gates = rg_ref[...]
    ff = gates[:, 0:1] * buf_ref[0] + gates[:, 1:2] * buf_ref[1]
    y = _layer_norm(alpha * res_ref[...] + ff, g_ref[...], b_ref[...])
    of_ref[...] = y
    ob_ref[...] = y.astype(BF16)


def _combine_ln(y, slots, gates, res, g, b, *, alpha):
    n, d = res.shape
    tm = _tile(n, 256)
    slots3 = slots.reshape(n // tm, 1, TOP_K * tm)
    vmem = 2 * (tm * V7X_LANES * 4 + tm * d * 4 + tm * d * 6) + TOP_K * tm * d * 4 + 3 * tm * d * 4
    return pl.pallas_call(
        functools.partial(_combine_ln_kernel, alpha=alpha),
        grid=(n // tm,),
        in_specs=[pl.BlockSpec((1, 1, TOP_K * tm), lambda i: (i, 0, 0), memory_space=pltpu.SMEM),
                  pl.BlockSpec(memory_space=pl.ANY),
                  pl.BlockSpec((tm, V7X_LANES), lambda i: (i, 0)),
                  pl.BlockSpec((tm, d), lambda i: (i, 0)),
                  pl.BlockSpec((1, d), lambda i: (0, 0)),
                  pl.BlockSpec((1, d), lambda i: (0, 0))],
        out_specs=[pl.BlockSpec((tm, d), lambda i: (i, 0)),
                   pl.BlockSpec((tm, d), lambda i: (i, 0))],
        out_shape=[jax.ShapeDtypeStruct((n, d), F32), jax.ShapeDtypeStruct((n, d), BF16)],
        scratch_shapes=[pltpu.VMEM((TOP_K, tm, d), F32), pltpu.SemaphoreType.DMA(())],
        compiler_params=_params(("arbitrary",), vmem),
        name="moe_combine_ln",
    )(slots3, y, gates, res, g, b)


def _moe_layer(x, x_packed, w_router, w_gate, w_up, w_down, layer, g, b, *, alpha):
    n, d = x.shape
    n_experts = w_router.shape[1]
    tm = MOE_TILE
    n_tiles = (TOP_K * n) // tm + n_experts
    wr = jnp.pad(w_router, ((0, 0), (0, V7X_LANES - n_experts)))
    ri, rg, cnt = _router(x, wr, n_experts=n_experts)

    counts = cnt[0, :n_experts]
    tiles_per = (counts + tm - 1) // tm
    tile_end = jnp.cumsum(tiles_per)
    n_used = tile_end[-1]
    group_start = (tile_end - tiles_per) * tm
    tile_id = jnp.arange(n_tiles, dtype=jnp.int32)
    last_used = jnp.minimum(tile_id, jnp.maximum(n_used - 1, 0))
    tile_expert = jnp.sum((last_used[:, None] >= tile_end[None, :]).astype(jnp.int32), axis=1)
    tile_expert = jnp.minimum(tile_expert, n_experts - 1)
    tile_first = jnp.concatenate([jnp.ones((1,), jnp.int32),
                                  (tile_expert[1:] != tile_expert[:-1]).astype(jnp.int32)])
    slots = (group_start[ri[:, :TOP_K]] + ri[:, TOP_K:2 * TOP_K]).astype(jnp.int32).reshape(-1)
    n_used = n_used.astype(jnp.int32).reshape(1)

    xs = _dispatch(x_packed, slots, n_slots=n_tiles * tm)
    h = _moe_up(xs, w_gate, w_up, layer, tile_expert, tile_first, n_used)
    y = _moe_down(h, w_down, layer, tile_expert, tile_first, n_used)
    return _combine_ln(y, slots, rg, x, g, b, alpha=alpha)


def kernel(x, conv_w_in, conv_w, conv_w_out, w_kv, attn_w_q, attn_sinks, attn_w_o, ln_g, ln_b,
           ffn_w_gate, ffn_w_up, ffn_w_down, moe_w_router, moe_w_gate, moe_w_up, moe_w_down):
    bsz, seq, d = x.shape
    n = bsz * seq
    depth = ln_g.shape[0]
    n_a = conv_w_in.shape[0]
    alpha = float((2 * depth) ** 0.25)
    assert seq % ATTN_BLOCK == 0 and d % (HEAD_DIM * GQA_GROUP) == 0
    assert w_kv.shape[1] == 2 * d // GQA_GROUP

    xf = x.reshape(n, d)
    xb = xf.astype(BF16)
    k_shared = vt_shared = None
    for l in range(depth):
        g0, b0 = ln_g[l, 0].reshape(1, d), ln_b[l, 0].reshape(1, d)
        g1, b1 = ln_g[l, 1].reshape(1, d), ln_b[l, 1].reshape(1, d)
        moe = l % 2 == 1
        if l < n_a:
            gated = _conv_gate(xb, conv_w_in, conv_w, l, seq=seq)
            xf, xb = _proj_ln(gated, conv_w_out[l].astype(BF16), xf, g0, b0, alpha=alpha,
                              name="conv_out_ln", packed=moe)
        else:
            j = l - n_a
            qt = _matmul_nt(attn_w_q[j].T.astype(BF16), xb, out_dtype=BF16,
                            scale=1.0 / math.sqrt(HEAD_DIM), name="attn_q")
            o = _attention(qt, k_shared, vt_shared, attn_sinks[j], bsz=bsz, seq=seq)
            xf, xb = _proj_ln(o, attn_w_o[j].astype(BF16), xf, g0, b0, alpha=alpha,
                              name="attn_out_ln", packed=moe)
        i = l // 2
        if moe:
            xf, xb = _moe_layer(xf, xb, moe_w_router[i], moe_w_gate, moe_w_up, moe_w_down, i,
                                g1, b1, alpha=alpha)
        else:
            h = _swiglu_up(xb, ffn_w_gate, ffn_w_up, i)
            xf, xb = _proj_ln(h, ffn_w_down[i].astype(BF16), xf, g1, b1, alpha=alpha, name="ffn_down_ln")
        if l == n_a - 1:
            kvd = w_kv.shape[1] // 2
            k_shared = _matmul(xb, w_kv[:, :kvd].astype(BF16), out_dtype=BF16, name="k_proj")
            vt_shared = _matmul_nt(w_kv[:, kvd:].T.astype(BF16), xb, out_dtype=BF16, name="v_proj")
    return xf.reshape(bsz, seq, d)
```

```python
import functools
import math

import jax
import jax.numpy as jnp
from jax import lax
from jax.experimental import pallas as pl
from jax.experimental.pallas import tpu as pltpu

F32 = jnp.float32
BF16 = jnp.bfloat16

HEAD_DIM = 64
GQA_GROUP = 8
ATTN_BLOCK = 128
TOP_K = 2
LN_EPS = 1e-5
NEG_INF = -1e30

V7X_LANES = 128
V7X_SUBLANES = 8
V7X_VMEM_BYTES = 64 * 1024 * 1024
V7X_VMEM_BUDGET = V7X_VMEM_BYTES - 8 * 1024 * 1024

MOE_TILE = 512
MOE_ROW_CHUNK = 128


def _tile(dim, target, quantum=V7X_LANES):
    if dim <= target:
        return dim
    t = (target // quantum) * quantum
    while t > quantum and dim % t:
        t -= quantum
    assert dim % t == 0, (dim, target)
    return t


def _params(semantics, vmem_bytes):
    limit = min(V7X_VMEM_BUDGET, max(32 * 1024 * 1024, int(vmem_bytes)))
    return pltpu.CompilerParams(dimension_semantics=semantics, vmem_limit_bytes=limit)


def _dot(a, b):
    return jnp.dot(a, b, preferred_element_type=F32)


def _layer_norm(z, g, b):
    mu = jnp.mean(z, axis=-1, keepdims=True)
    zc = z - mu
    var = jnp.mean(zc * zc, axis=-1, keepdims=True)
    return zc * lax.rsqrt(var + LN_EPS) * g + b


def _matmul_kernel(x_ref, w_ref, o_ref, *, scale):
    acc = _dot(x_ref[...], w_ref[...])
    if scale != 1.0:
        acc = acc * scale
    o_ref[...] = acc.astype(o_ref.dtype)


def _matmul(x, w, *, out_dtype, scale=1.0, name):
    n, k = x.shape
    m = w.shape[1]
    tm = _tile(n, 1024)
    tn = _tile(m, 1024)
    vmem = 2 * (tm * k * 2 + k * tn * 2 + tm * tn * 4) + tm * tn * 4
    return pl.pallas_call(
        functools.partial(_matmul_kernel, scale=scale),
        grid=(m // tn, n // tm),
        in_specs=[pl.BlockSpec((tm, k), lambda j, i: (i, 0)),
                  pl.BlockSpec((k, tn), lambda j, i: (0, j))],
        out_specs=pl.BlockSpec((tm, tn), lambda j, i: (i, j)),
        out_shape=jax.ShapeDtypeStruct((n, m), out_dtype),
        compiler_params=_params(("parallel", "parallel"), vmem),
        name=name,
    )(x, w)


def _matmul_nt_kernel(wt_ref, x_ref, o_ref, *, scale):
    acc = lax.dot_general(wt_ref[...], x_ref[...], (((1,), (1,)), ((), ())),
                          preferred_element_type=F32)
    if scale != 1.0:
        acc = acc * scale
    o_ref[...] = acc.astype(o_ref.dtype)


def _matmul_nt(wt, x, *, out_dtype, scale=1.0, name):
    m, k = wt.shape
    n = x.shape[0]
    tm = _tile(n, 1024)
    tn = _tile(m, 1024)
    vmem = 2 * (tm * k * 2 + k * tn * 2 + tm * tn * 4) + tm * tn * 4
    return pl.pallas_call(
        functools.partial(_matmul_nt_kernel, scale=scale),
        grid=(m // tn, n // tm),
        in_specs=[pl.BlockSpec((tn, k), lambda j, i: (j, 0)),
                  pl.BlockSpec((tm, k), lambda j, i: (i, 0))],
        out_specs=pl.BlockSpec((tn, tm), lambda j, i: (j, i)),
        out_shape=jax.ShapeDtypeStruct((m, n), out_dtype),
        compiler_params=_params(("parallel", "parallel"), vmem),
        name=name,
    )(wt, x)


def _conv_gate_kernel(x_ref, wb_ref, wc_ref, wh_ref, cw_ref, o_ref, w16_ref, ubuf_ref, *, tiles_per_seq):
    i = pl.program_id(1)
    tm = x_ref.shape[0]
    pad = V7X_SUBLANES

    @pl.when(i == 0)
    def _():
        w16_ref[0] = wb_ref[...].astype(BF16)
        w16_ref[1] = wc_ref[...].astype(BF16)
        w16_ref[2] = wh_ref[...].astype(BF16)

    @pl.when(i % tiles_per_seq == 0)
    def _():
        ubuf_ref[0:pad, :] = jnp.zeros((pad, ubuf_ref.shape[1]), F32)

    x = x_ref[...]
    u = _dot(x, w16_ref[1]) * _dot(x, w16_ref[2])
    ubuf_ref[pad:pad + tm, :] = u
    u1 = ubuf_ref[pad - 1:pad - 1 + tm, :]
    u2 = ubuf_ref[pad - 2:pad - 2 + tm, :]
    cw = cw_ref[...]
    conv = cw[0:1, :] * u2 + cw[1:2, :] * u1 + cw[2:3, :] * u
    o_ref[...] = (_dot(x, w16_ref[0]) * conv).astype(o_ref.dtype)
    ubuf_ref[0:pad, :] = ubuf_ref[tm:tm + pad, :]


def _conv_gate(xb, w_in, conv_w, layer, *, seq):
    n, d = xb.shape
    tm = _tile(seq, 1024)
    tn = _tile(d, 256)
    nj = d // tn
    vmem = (2 * (tm * d * 2 + 3 * d * tn * 4 + tm * tn * 2) + 3 * d * tn * 2
            + (tm + 8) * tn * 4 + 6 * tm * tn * 4)
    w_spec = lambda third: pl.BlockSpec((None, d, tn), lambda j, i: (layer, 0, third * nj + j))
    return pl.pallas_call(
        functools.partial(_conv_gate_kernel, tiles_per_seq=seq // tm),
        grid=(nj, n // tm),
        in_specs=[pl.BlockSpec((tm, d), lambda j, i: (i, 0)),
                  w_spec(0), w_spec(1), w_spec(2),
                  pl.BlockSpec((None, 3, tn), lambda j, i: (layer, 0, j))],
        out_specs=pl.BlockSpec((tm, tn), lambda j, i: (i, j)),
        out_shape=jax.ShapeDtypeStruct((n, d), BF16),
        scratch_shapes=[pltpu.VMEM((3, d, tn), BF16), pltpu.VMEM((tm + V7X_SUBLANES, tn), F32)],
        compiler_params=_params(("arbitrary", "arbitrary"), vmem),
        name="conv_gate",
    )(xb, w_in, w_in, w_in, conv_w)


def _pack_bf16_pairs(y):
    half = y.shape[1] // 2
    lo = lax.bitcast_convert_type(y[:, :half].astype(BF16).astype(F32), jnp.uint32)
    hi = lax.bitcast_convert_type(y[:, half:].astype(BF16).astype(F32), jnp.uint32)
    return (lo >> 16) | (hi & jnp.uint32(0xFFFF0000))


def _unpack_bf16_pairs(p):
    lo = lax.bitcast_convert_type(p << 16, F32).astype(BF16)
    hi = lax.bitcast_convert_type(p & jnp.uint32(0xFFFF0000), F32).astype(BF16)
    return lo, hi


def _proj_ln_kernel(a_ref, w_ref, res_ref, g_ref, b_ref, of_ref, ob_ref, *, alpha, nk, row_chunks, packed):
    k = pl.program_id(1)
    rc = a_ref.shape[0] // row_chunks

    def finish(rows, acc):
        y = _layer_norm(alpha * res_ref[rows, :] + acc, g_ref[...], b_ref[...])
        of_ref[rows, :] = y
        ob_ref[rows, :] = _pack_bf16_pairs(y) if packed else y.astype(BF16)

    for c in range(row_chunks):
        rows = pl.ds(c * rc, rc)
        part = _dot(a_ref[rows, :], w_ref[...])
        if nk == 1:
            finish(rows, part)
            continue

        @pl.when(k == 0)
        def _():
            of_ref[rows, :] = part

        @pl.when(jnp.logical_and(k > 0, k < nk - 1))
        def _():
            of_ref[rows, :] += part

        @pl.when(k == nk - 1)
        def _():
            finish(rows, of_ref[rows, :] + part)


def _proj_ln(a, w, res, g, b, *, alpha, name, packed=False):
    n, kdim = a.shape
    d = w.shape[1]
    split = kdim > 2048
    tm = _tile(n, 512)
    tk = _tile(kdim, 1536) if split else kdim
    nk = kdim // tk
    row_chunks = 1 if split else 2
    vmem = 2 * (tm * tk * 2 + tk * d * 2 + tm * d * 4 + tm * d * 6) + 3 * (tm // row_chunks) * d * 4
    ob_shape = jax.ShapeDtypeStruct((n, d // 2), jnp.uint32) if packed else jax.ShapeDtypeStruct((n, d), BF16)
    return pl.pallas_call(
        functools.partial(_proj_ln_kernel, alpha=alpha, nk=nk, row_chunks=row_chunks, packed=packed),
        grid=(n // tm, nk),
        in_specs=[pl.BlockSpec((tm, tk), lambda i, k: (i, k)),
                  pl.BlockSpec((tk, d), lambda i, k: (k, 0)),
                  pl.BlockSpec((tm, d), lambda i, k: (i, 0)),
                  pl.BlockSpec((1, d), lambda i, k: (0, 0)),
                  pl.BlockSpec((1, d), lambda i, k: (0, 0))],
        out_specs=[pl.BlockSpec((tm, d), lambda i, k: (i, 0)),
                   pl.BlockSpec((tm, ob_shape.shape[1]), lambda i, k: (i, 0))],
        out_shape=[jax.ShapeDtypeStruct((n, d), F32), ob_shape],
        compiler_params=_params(("parallel", "arbitrary"), vmem),
        name=name,
    )(a, w, res, g, b)


def _swiglu_up_kernel(x_ref, wg_ref, wu_ref, o_ref, wg16_ref, wu16_ref):
    @pl.when(pl.program_id(1) == 0)
    def _():
        wg16_ref[...] = wg_ref[...].astype(BF16)
        wu16_ref[...] = wu_ref[...].astype(BF16)

    x = x_ref[...]
    o_ref[...] = (jax.nn.silu(_dot(x, wg16_ref[...])) * _dot(x, wu16_ref[...])).astype(o_ref.dtype)


def _swiglu_up(xb, wg, wu, layer):
    n, d = xb.shape
    f = wg.shape[2]
    tm = _tile(n, 1024)
    tf = _tile(f, 512)
    vmem = 2 * (tm * d * 2 + 2 * d * tf * 4 + tm * tf * 2) + 2 * d * tf * 2 + 4 * tm * tf * 4
    w_spec = pl.BlockSpec((None, d, tf), lambda j, i: (layer, 0, j))
    return pl.pallas_call(
        _swiglu_up_kernel,
        grid=(f // tf, n // tm),
        in_specs=[pl.BlockSpec((tm, d), lambda j, i: (i, 0)), w_spec, w_spec],
        out_specs=pl.BlockSpec((tm, tf), lambda j, i: (i, j)),
        out_shape=jax.ShapeDtypeStruct((n, f), BF16),
        scratch_shapes=[pltpu.VMEM((d, tf), BF16), pltpu.VMEM((d, tf), BF16)],
        compiler_params=_params(("arbitrary", "arbitrary"), vmem),
        name="swiglu_up",
    )(xb, wg, wu)


def _attn_kernel(sink_ref, qt_ref, kp_ref, kc_ref, vtp_ref, vtc_ref, o_ref, *, n_heads):
    has_prev = pl.program_id(1) > 0
    blk = o_ref.shape[0]
    key = lax.broadcasted_iota(jnp.int32, (blk, blk), 0)
    qry = lax.broadcasted_iota(jnp.int32, (blk, blk), 1)
    mask_c = key <= qry
    mask_p = jnp.logical_and(key > qry, has_prev)
    heads_per_store = V7X_LANES // HEAD_DIM
    done = []
    for h in range(n_heads):
        qs = slice(h * HEAD_DIM, (h + 1) * HEAD_DIM)
        ks = slice((h // GQA_GROUP) * HEAD_DIM, (h // GQA_GROUP + 1) * HEAD_DIM)
        qt = qt_ref[qs, :]
        s_c = jnp.where(mask_c, _dot(kc_ref[:, ks], qt), NEG_INF)
        s_p = jnp.where(mask_p, _dot(kp_ref[:, ks], qt), NEG_INF)
        sink = sink_ref[h]
        m = jnp.maximum(jnp.max(jnp.maximum(s_c, s_p), axis=0, keepdims=True), sink)
        p_c = jnp.exp(s_c - m)
        p_p = jnp.exp(s_p - m)
        denom = jnp.sum(p_c + p_p, axis=0, keepdims=True) + jnp.exp(sink - m)
        ot = _dot(vtc_ref[ks, :], p_c.astype(BF16)) + _dot(vtp_ref[ks, :], p_p.astype(BF16))
        done.append(ot / denom)
        if len(done) == heads_per_store:
            first = h + 1 - heads_per_store
            o_ref[:, first * HEAD_DIM:(h + 1) * HEAD_DIM] = (
                jnp.concatenate(done, axis=0).T.astype(o_ref.dtype))
            done = []


def _attention(qt, k, vt, sinks, *, bsz, seq):
    d, n = qt.shape
    kvd = k.shape[1]
    nb = seq // ATTN_BLOCK
    cur_row = lambda b, i: (b * nb + i, 0)
    prev_row = lambda b, i: (b * nb + jnp.maximum(i - 1, 0), 0)
    cur_col = lambda b, i: (0, b * nb + i)
    prev_col = lambda b, i: (0, b * nb + jnp.maximum(i - 1, 0))
    vmem = 2 * (2 * ATTN_BLOCK * d * 2 + 4 * ATTN_BLOCK * kvd * 2) + 64 * ATTN_BLOCK * ATTN_BLOCK * 4
    return pl.pallas_call(
        functools.partial(_attn_kernel, n_heads=d // HEAD_DIM),
        grid=(bsz, nb),
        in_specs=[pl.BlockSpec(memory_space=pltpu.SMEM),
                  pl.BlockSpec((d, ATTN_BLOCK), cur_col),
                  pl.BlockSpec((ATTN_BLOCK, kvd), prev_row),
                  pl.BlockSpec((ATTN_BLOCK, kvd), cur_row),
                  pl.BlockSpec((kvd, ATTN_BLOCK), prev_col),
                  pl.BlockSpec((kvd, ATTN_BLOCK), cur_col)],
        out_specs=pl.BlockSpec((ATTN_BLOCK, d), cur_row),
        out_shape=jax.ShapeDtypeStruct((n, d), BF16),
        compiler_params=_params(("parallel", "parallel"), vmem),
        name="swa_attention",
    )(sinks, qt, k, k, vt, vt)


def _router_kernel(x_ref, wr_ref, ri_ref, rg_ref, cnt_ref, carry_ref, *, n_experts):
    i = pl.program_id(0)
    tm = x_ref.shape[0]

    @pl.when(i == 0)
    def _():
        carry_ref[...] = jnp.zeros_like(carry_ref)

    logits = jnp.dot(x_ref[...], wr_ref[...], preferred_element_type=F32,
                     precision=lax.Precision.HIGHEST)
    lane = lax.broadcasted_iota(jnp.int32, logits.shape, 1)
    logits = jnp.where(lane < n_experts, logits, -jnp.inf)
    m1 = jnp.max(logits, axis=-1, keepdims=True)
    i1 = jnp.min(jnp.where(logits == m1, lane, V7X_LANES), axis=-1, keepdims=True)
    rest = jnp.where(lane == i1, -jnp.inf, logits)
    m2 = jnp.max(rest, axis=-1, keepdims=True)
    i2 = jnp.min(jnp.where(rest == m2, lane, V7X_LANES), axis=-1, keepdims=True)
    e2 = jnp.exp(m2 - m1)
    g1 = 1.0 / (1.0 + e2)
    g2 = e2 / (1.0 + e2)

    sel1 = lane == i1
    sel2 = lane == i2
    member = jnp.logical_or(sel1, sel2).astype(BF16)
    r = lax.broadcasted_iota(jnp.int32, (tm, tm), 0)
    c = lax.broadcasted_iota(jnp.int32, (tm, tm), 1)
    before = (c < r).astype(BF16)
    prior = carry_ref[...] + _dot(before, member)
    r1 = jnp.sum(jnp.where(sel1, prior, 0.0), axis=-1, keepdims=True).astype(jnp.int32)
    r2 = jnp.sum(jnp.where(sel2, prior, 0.0), axis=-1, keepdims=True).astype(jnp.int32)
    carry_ref[...] += jnp.sum(member.astype(F32), axis=0, keepdims=True)

    ri_ref[...] = jnp.where(lane == 0, i1, jnp.where(lane == 1, i2,
                            jnp.where(lane == 2, r1, jnp.where(lane == 3, r2, 0))))
    rg_ref[...] = jnp.where(lane == 0, g1, jnp.where(lane == 1, g2, 0.0))
    cnt_ref[...] = carry_ref[...].astype(jnp.int32)


def _router(x, w_router_padded, *, n_experts):
    n, d = x.shape
    tm = _tile(n, 512)
    vmem = 2 * (tm * d * 4 + d * V7X_LANES * 4 + 2 * tm * V7X_LANES * 4) + 8 * tm * tm * 4
    return pl.pallas_call(
        functools.partial(_router_kernel, n_experts=n_experts),
        grid=(n // tm,),
        in_specs=[pl.BlockSpec((tm, d), lambda i: (i, 0)),
                  pl.BlockSpec((d, V7X_LANES), lambda i: (0, 0))],
        out_specs=[pl.BlockSpec((tm, V7X_LANES), lambda i: (i, 0)),
                   pl.BlockSpec((tm, V7X_LANES), lambda i: (i, 0)),
                   pl.BlockSpec((1, V7X_LANES), lambda i: (0, 0))],
        out_shape=[jax.ShapeDtypeStruct((n, V7X_LANES), jnp.int32),
                   jax.ShapeDtypeStruct((n, V7X_LANES), F32),
                   jax.ShapeDtypeStruct((1, V7X_LANES), jnp.int32)],
        scratch_shapes=[pltpu.VMEM((1, V7X_LANES), F32)],
        compiler_params=_params(("arbitrary",), vmem),
        name="moe_router",
    )(x, w_router_padded)


def _dispatch_kernel(slot_ref, x_ref, init_hbm, xs_hbm, sem):
    del init_hbm
    chunk = x_ref.shape[0]

    def copies(r):
        return [pltpu.make_async_copy(x_ref.at[pl.ds(r, 1), :],
                                      xs_hbm.at[pl.ds(slot_ref[0, 0, TOP_K * r + k], 1), :], sem)
                for k in range(TOP_K)]

    def start(r, carry):
        for cp in copies(r):
            cp.start()
        return carry

    lax.fori_loop(0, chunk, start, 0)
    for _ in range(TOP_K):
        pltpu.make_async_copy(x_ref, xs_hbm.at[pl.ds(0, chunk), :], sem).wait()


def _dispatch(x, slots, *, n_slots):
    n, d = x.shape
    chunk = _tile(n, 512)
    slots3 = slots.reshape(n // chunk, 1, TOP_K * chunk)
    init = jnp.zeros((n_slots, d), x.dtype)
    return pl.pallas_call(
        _dispatch_kernel,
        grid=(n // chunk,),
        in_specs=[pl.BlockSpec((1, 1, TOP_K * chunk), lambda i: (i, 0, 0), memory_space=pltpu.SMEM),
                  pl.BlockSpec((chunk, d), lambda i: (i, 0)),
                  pl.BlockSpec(memory_space=pl.ANY)],
        out_specs=pl.BlockSpec(memory_space=pl.ANY),
        out_shape=jax.ShapeDtypeStruct((n_slots, d), x.dtype),
        scratch_shapes=[pltpu.SemaphoreType.DMA(())],
        input_output_aliases={2: 0},
        compiler_params=_params(("arbitrary",), 2 * chunk * d * 4),
        name="moe_dispatch",
    )(slots3, x, init)


def _for_valid_rows(n_valid, tm, compute, o_ref):
    @pl.when(n_valid == tm)
    def _():
        compute(pl.ds(0, tm))

    for c in range(tm // MOE_ROW_CHUNK):
        rows = pl.ds(c * MOE_ROW_CHUNK, MOE_ROW_CHUNK)

        @pl.when(jnp.logical_and(n_valid < tm, n_valid > c * MOE_ROW_CHUNK))
        def _():
            compute(rows)

        @pl.when(n_valid <= c * MOE_ROW_CHUNK)
        def _():
            o_ref[rows, :] = jnp.zeros((MOE_ROW_CHUNK, o_ref.shape[1]), o_ref.dtype)


def _moe_up_kernel(te_ref, first_ref, rows_ref, xs_ref, wg_ref, wu_ref, o_ref, wgb_ref, wub_ref):
    del te_ref
    t = pl.program_id(1)
    tm, half = xs_ref.shape

    @pl.when(first_ref[t] == 1)
    def _():
        wgb_ref[...] = wg_ref[...].astype(BF16)
        wub_ref[...] = wu_ref[...].astype(BF16)

    def compute(rows):
        lo, hi = _unpack_bf16_pairs(xs_ref[rows, :])
        gate = _dot(lo, wgb_ref[0:half, :]) + _dot(hi, wgb_ref[half:2 * half, :])
        up = _dot(lo, wub_ref[0:half, :]) + _dot(hi, wub_ref[half:2 * half, :])
        o_ref[rows, :] = (jax.nn.silu(gate) * up).astype(o_ref.dtype)

    _for_valid_rows(rows_ref[t], tm, compute, o_ref)


def _moe_up(xs, w_gate, w_up, layer, tile_expert, tile_first, tile_rows):
    s = xs.shape[0]
    d = 2 * xs.shape[1]
    f = w_gate.shape[3]
    tm = MOE_TILE
    tf = _tile(f, 1024)
    vmem = 2 * (tm * d * 2 + 2 * d * tf * 4 + tm * tf * 2) + 2 * d * tf * 2 + 3 * tm * tf * 4
    w_spec = pl.BlockSpec((None, None, d, tf), lambda j, t, te, fi, nu: (layer, te[t], 0, j))
    grid_spec = pltpu.PrefetchScalarGridSpec(
        num_scalar_prefetch=3,
        grid=(f // tf, s // tm),
        in_specs=[pl.BlockSpec((tm, d // 2), lambda j, t, te, fi, nu: (t, 0)), w_spec, w_spec],
        out_specs=pl.BlockSpec((tm, tf), lambda j, t, te, fi, nu: (t, j)),
        scratch_shapes=[pltpu.VMEM((d, tf), BF16), pltpu.VMEM((d, tf), BF16)],
    )
    return pl.pallas_call(
        _moe_up_kernel,
        grid_spec=grid_spec,
        out_shape=jax.ShapeDtypeStruct((s, f), BF16),
        compiler_params=_params(("arbitrary", "arbitrary"), vmem),
        name="moe_up",
    )(tile_expert, tile_first, tile_rows, xs, w_gate, w_up)


def _moe_down_kernel(te_ref, first_ref, rows_ref, h_ref, wd_ref, o_ref, wdb_ref):
    del te_ref
    t = pl.program_id(1)

    @pl.when(first_ref[t] == 1)
    def _():
        wdb_ref[...] = wd_ref[...].astype(BF16)

    def compute(rows):
        o_ref[rows, :] = _dot(h_ref[rows, :], wdb_ref[...])

    _for_valid_rows(rows_ref[t], h_ref.shape[0], compute, o_ref)


def _moe_down(h, w_down, layer, tile_expert, tile_first, tile_rows):
    s, f = h.shape
    d = w_down.shape[3]
    tm = MOE_TILE
    tn = _tile(d, 512)
    vmem = 2 * (tm * f * 2 + tm * tn * 4 + f * tn * 4) + f * tn * 2 + 2 * tm * tn * 4
    grid_spec = pltpu.PrefetchScalarGridSpec(
        num_scalar_prefetch=3,
        grid=(d // tn, s // tm),
        in_specs=[pl.BlockSpec((tm, f), lambda j, t, te, fi, nu: (t, 0)),
                  pl.BlockSpec((None, None, f, tn), lambda j, t, te, fi, nu: (layer, te[t], 0, j))],
        out_specs=pl.BlockSpec((tm, tn), lambda j, t, te, fi, nu: (t, j)),
        scratch_shapes=[pltpu.VMEM((f, tn), BF16)],
    )
    return pl.pallas_call(
        _moe_down_kernel,
        grid_spec=grid_spec,
        out_shape=jax.ShapeDtypeStruct((s, d), F32),
        compiler_params=_params(("arbitrary", "arbitrary"), vmem),
        name="moe_down",
    )(tile_expert, tile_first, tile_rows, h, w_down)


def _combine_ln_kernel(slot_ref, next_slot_ref, y_hbm, rg_ref, res_ref, g_ref, b_ref, of_ref, ob_ref,
                       buf_ref, sem, *, alpha):
    i = pl.program_id(0)
    tm = res_ref.shape[0]

    def gather(slots, parity):
        def start(r, carry):
            for k in range(TOP_K):
                pltpu.make_async_copy(y_hbm.at[pl.ds(slots[0, 0, TOP_K * r + k], 1), :],
                                      buf_ref.at[parity * TOP_K + k, pl.ds(r, 1), :],
                                      sem.at[parity]).start()
            return carry
        lax.fori_loop(0, tm, start, 0)

    @pl.when(i == 0)
    def _():
        gather(slot_ref, 0)

    @pl.when(i + 1 < pl.num_programs(0))
    def _():
        gather(next_slot_ref, (i + 1) % 2)

    parity = i % 2
    for k in range(TOP_K):
        pltpu.make_async_copy(y_hbm.at[pl.ds(0, tm), :], buf_ref.at[parity * TOP_K + k],
                              sem.at[parity]).wait()
    gates = rg_ref[...]
    ff = gates[:, 0:1] * buf_ref[parity * TOP_K] + gates[:, 1:2] * buf_ref[parity * TOP_K + 1]
    y = _layer_norm(alpha * res_ref[...] + ff, g_ref[...], b_ref[...])
    of_ref[...] = y
    ob_ref[...] = y.astype(BF16)


def _combine_ln(y, slots, gates, res, g, b, *, alpha):
    n, d = res.shape
    tm = _tile(n, 256)
    steps = n // tm
    slots3 = slots.reshape(steps, 1, TOP_K * tm)
    vmem = (2 * (tm * V7X_LANES * 4 + tm * d * 4 + tm * d * 6) + 2 * TOP_K * tm * d * 4
            + 3 * tm * d * 4)
    slot_block = (1, 1, TOP_K * tm)
    return pl.pallas_call(
        functools.partial(_combine_ln_kernel, alpha=alpha),
        grid=(steps,),
        in_specs=[pl.BlockSpec(slot_block, lambda i: (i, 0, 0), memory_space=pltpu.SMEM),
                  pl.BlockSpec(slot_block, lambda i: (jnp.minimum(i + 1, steps - 1), 0, 0),
                               memory_space=pltpu.SMEM),
                  pl.BlockSpec(memory_space=pl.ANY),
                  pl.BlockSpec((tm, V7X_LANES), lambda i: (i, 0)),
                  pl.BlockSpec((tm, d), lambda i: (i, 0)),
                  pl.BlockSpec((1, d), lambda i: (0, 0)),
                  pl.BlockSpec((1, d), lambda i: (0, 0))],
        out_specs=[pl.BlockSpec((tm, d), lambda i: (i, 0)),
                   pl.BlockSpec((tm, d), lambda i: (i, 0))],
        out_shape=[jax.ShapeDtypeStruct((n, d), F32), jax.ShapeDtypeStruct((n, d), BF16)],
        scratch_shapes=[pltpu.VMEM((2 * TOP_K, tm, d), F32), pltpu.SemaphoreType.DMA((2,))],
        compiler_params=_params(("arbitrary",), vmem),
        name="moe_combine_ln",
    )(slots3, slots3, y, gates, res, g, b)


def _moe_layer(x, x_packed, w_router, w_gate, w_up, w_down, layer, g, b, *, alpha):
    n, d = x.shape
    n_experts = w_router.shape[1]
    tm = MOE_TILE
    n_tiles = (TOP_K * n) // tm + n_experts
    wr = jnp.pad(w_router, ((0, 0), (0, V7X_LANES - n_experts)))
    ri, rg, cnt = _router(x, wr, n_experts=n_experts)

    counts = cnt[0, :n_experts]
    tiles_per = (counts + tm - 1) // tm
    tile_end = jnp.cumsum(tiles_per)
    n_used = tile_end[-1]
    group_start = (tile_end - tiles_per) * tm
    tile_id = jnp.arange(n_tiles, dtype=jnp.int32)
    last_used = jnp.minimum(tile_id, jnp.maximum(n_used - 1, 0))
    tile_expert = jnp.sum((last_used[:, None] >= tile_end[None, :]).astype(jnp.int32), axis=1)
    tile_expert = jnp.minimum(tile_expert, n_experts - 1)
    tile_first = jnp.concatenate([jnp.ones((1,), jnp.int32),
                                  (tile_expert[1:] != tile_expert[:-1]).astype(jnp.int32)])
    slots = (group_start[ri[:, :TOP_K]] + ri[:, TOP_K:2 * TOP_K]).astype(jnp.int32).reshape(-1)
    group_end = group_start + counts
    tile_rows = jnp.clip(group_end[tile_expert] - tile_id * tm, 0, tm)
    tile_rows = jnp.where(tile_id < n_used, tile_rows, 0).astype(jnp.int32)

    xs = _dispatch(x_packed, slots, n_slots=n_tiles * tm)
    h = _moe_up(xs, w_gate, w_up, layer, tile_expert, tile_first, tile_rows)
    y = _moe_down(h, w_down, layer, tile_expert, tile_first, tile_rows)
    return _combine_ln(y, slots, rg, x, g, b, alpha=alpha)


def kernel(x, conv_w_in, conv_w, conv_w_out, w_kv, attn_w_q, attn_sinks, attn_w_o, ln_g, ln_b,
           ffn_w_gate, ffn_w_up, ffn_w_down, moe_w_router, moe_w_gate, moe_w_up, moe_w_down):
    bsz, seq, d = x.shape
    n = bsz * seq
    depth = ln_g.shape[0]
    n_a = conv_w_in.shape[0]
    alpha = float((2 * depth) ** 0.25)
    assert seq % ATTN_BLOCK == 0 and d % (HEAD_DIM * GQA_GROUP) == 0
    assert w_kv.shape[1] == 2 * d // GQA_GROUP

    xf = x.reshape(n, d)
    xb = xf.astype(BF16)
    k_shared = vt_shared = None
    for l in range(depth):
        g0, b0 = ln_g[l, 0].reshape(1, d), ln_b[l, 0].reshape(1, d)
        g1, b1 = ln_g[l, 1].reshape(1, d), ln_b[l, 1].reshape(1, d)
        moe = l % 2 == 1
        if l < n_a:
            gated = _conv_gate(xb, conv_w_in, conv_w, l, seq=seq)
            xf, xb = _proj_ln(gated, conv_w_out[l].astype(BF16), xf, g0, b0, alpha=alpha,
                              name="conv_out_ln", packed=moe)
        else:
            j = l - n_a
            qt = _matmul_nt(attn_w_q[j].T.astype(BF16), xb, out_dtype=BF16,
                            scale=1.0 / math.sqrt(HEAD_DIM), name="attn_q")
            o = _attention(qt, k_shared, vt_shared, attn_sinks[j], bsz=bsz, seq=seq)
            xf, xb = _proj_ln(o, attn_w_o[j].astype(BF16), xf, g0, b0, alpha=alpha,
                              name="attn_out_ln", packed=moe)
        i = l // 2
        if moe:
            xf, xb = _moe_layer(xf, xb, moe_w_router[i], moe_w_gate, moe_w_up, moe_w_down, i,
                                g1, b1, alpha=alpha)
        else:
            h = _swiglu_up(xb, ffn_w_gate, ffn_w_up, i)
            xf, xb = _proj_ln(h, ffn_w_down[i].astype(BF16), xf, g1, b1, alpha=alpha, name="ffn_down_ln")
        if l == n_a - 1:
            kvd = w_kv.shape[1] // 2
            k_shared = _matmul(xb, w_kv[:, :kvd].astype(BF16), out_dtype=BF16, name="k_proj")
            vt_shared = _matmul_nt(w_kv[:, kvd:].T.astype(BF16), xb, out_dtype=BF16, name="v_proj")
    return xf.reshape(bsz, seq, d)
```

```python
import functools
import math

import jax
import jax.numpy as jnp
from jax import lax
from jax.experimental import pallas as pl
from jax.experimental.pallas import tpu as pltpu

F32 = jnp.float32
BF16 = jnp.bfloat16

HEAD_DIM = 64
GQA_GROUP = 8
ATTN_BLOCK = 128
TOP_K = 2
LN_EPS = 1e-5
NEG_INF = -1e30

V7X_LANES = 128
V7X_SUBLANES = 8
V7X_VMEM_BYTES = 64 * 1024 * 1024
V7X_VMEM_BUDGET = V7X_VMEM_BYTES - 8 * 1024 * 1024

MOE_TILE = 512
DMA_ISSUE_UNROLL = 8


def _tile(dim, target, quantum=V7X_LANES):
    if dim <= target:
        return dim
    t = (target // quantum) * quantum
    while t > quantum and dim % t:
        t -= quantum
    assert dim % t == 0, (dim, target)
    return t


def _params(semantics, vmem_bytes):
    limit = min(V7X_VMEM_BUDGET, max(32 * 1024 * 1024, int(vmem_bytes)))
    return pltpu.CompilerParams(dimension_semantics=semantics, vmem_limit_bytes=limit)


def _dot(a, b):
    return jnp.dot(a, b, preferred_element_type=F32)


def _layer_norm(z, g, b):
    mu = jnp.mean(z, axis=-1, keepdims=True)
    zc = z - mu
    var = jnp.mean(zc * zc, axis=-1, keepdims=True)
    return zc * lax.rsqrt(var + LN_EPS) * g + b


def _matmul_kernel(x_ref, w_ref, o_ref, *, scale):
    acc = _dot(x_ref[...], w_ref[...])
    if scale != 1.0:
        acc = acc * scale
    o_ref[...] = acc.astype(o_ref.dtype)


def _matmul(x, w, *, out_dtype, scale=1.0, name):
    n, k = x.shape
    m = w.shape[1]
    tm = _tile(n, 1024)
    tn = _tile(m, 1024)
    vmem = 2 * (tm * k * 2 + k * tn * 2 + tm * tn * 4) + tm * tn * 4
    return pl.pallas_call(
        functools.partial(_matmul_kernel, scale=scale),
        grid=(m // tn, n // tm),
        in_specs=[pl.BlockSpec((tm, k), lambda j, i: (i, 0)),
                  pl.BlockSpec((k, tn), lambda j, i: (0, j))],
        out_specs=pl.BlockSpec((tm, tn), lambda j, i: (i, j)),
        out_shape=jax.ShapeDtypeStruct((n, m), out_dtype),
        compiler_params=_params(("parallel", "parallel"), vmem),
        name=name,
    )(x, w)


def _matmul_nt_kernel(wt_ref, x_ref, o_ref, *, scale):
    acc = lax.dot_general(wt_ref[...], x_ref[...], (((1,), (1,)), ((), ())),
                          preferred_element_type=F32)
    if scale != 1.0:
        acc = acc * scale
    o_ref[...] = acc.astype(o_ref.dtype)


def _matmul_nt(wt, x, *, out_dtype, scale=1.0, name):
    m, k = wt.shape
    n = x.shape[0]
    tm = _tile(n, 1024)
    tn = _tile(m, 1024)
    vmem = 2 * (tm * k * 2 + k * tn * 2 + tm * tn * 4) + tm * tn * 4
    return pl.pallas_call(
        functools.partial(_matmul_nt_kernel, scale=scale),
        grid=(m // tn, n // tm),
        in_specs=[pl.BlockSpec((tn, k), lambda j, i: (j, 0)),
                  pl.BlockSpec((tm, k), lambda j, i: (i, 0))],
        out_specs=pl.BlockSpec((tn, tm), lambda j, i: (j, i)),
        out_shape=jax.ShapeDtypeStruct((m, n), out_dtype),
        compiler_params=_params(("parallel", "parallel"), vmem),
        name=name,
    )(wt, x)


def _conv_gate_kernel(x_ref, wb_ref, wc_ref, wh_ref, cw_ref, o_ref, w16_ref, ubuf_ref, *, tiles_per_seq):
    i = pl.program_id(1)
    tm = x_ref.shape[0]
    pad = V7X_SUBLANES

    @pl.when(i == 0)
    def _():
        w16_ref[0] = wb_ref[...].astype(BF16)
        w16_ref[1] = wc_ref[...].astype(BF16)
        w16_ref[2] = wh_ref[...].astype(BF16)

    @pl.when(i % tiles_per_seq == 0)
    def _():
        ubuf_ref[0:pad, :] = jnp.zeros((pad, ubuf_ref.shape[1]), F32)

    x = x_ref[...]
    u = _dot(x, w16_ref[1]) * _dot(x, w16_ref[2])
    ubuf_ref[pad:pad + tm, :] = u
    u1 = ubuf_ref[pad - 1:pad - 1 + tm, :]
    u2 = ubuf_ref[pad - 2:pad - 2 + tm, :]
    cw = cw_ref[...]
    conv = cw[0:1, :] * u2 + cw[1:2, :] * u1 + cw[2:3, :] * u
    o_ref[...] = (_dot(x, w16_ref[0]) * conv).astype(o_ref.dtype)
    ubuf_ref[0:pad, :] = ubuf_ref[tm:tm + pad, :]


def _conv_gate(xb, w_in, conv_w, layer, *, seq):
    n, d = xb.shape
    tm = _tile(seq, 1024)
    tn = _tile(d, 256)
    nj = d // tn
    vmem = (2 * (tm * d * 2 + 3 * d * tn * 4 + tm * tn * 2) + 3 * d * tn * 2
            + (tm + 8) * tn * 4 + 6 * tm * tn * 4)
    w_spec = lambda third: pl.BlockSpec((None, d, tn), lambda j, i: (layer, 0, third * nj + j))
    return pl.pallas_call(
        functools.partial(_conv_gate_kernel, tiles_per_seq=seq // tm),
        grid=(nj, n // tm),
        in_specs=[pl.BlockSpec((tm, d), lambda j, i: (i, 0)),
                  w_spec(0), w_spec(1), w_spec(2),
                  pl.BlockSpec((None, 3, tn), lambda j, i: (layer, 0, j))],
        out_specs=pl.BlockSpec((tm, tn), lambda j, i: (i, j)),
        out_shape=jax.ShapeDtypeStruct((n, d), BF16),
        scratch_shapes=[pltpu.VMEM((3, d, tn), BF16), pltpu.VMEM((tm + V7X_SUBLANES, tn), F32)],
        compiler_params=_params(("arbitrary", "arbitrary"), vmem),
        name="conv_gate",
    )(xb, w_in, w_in, w_in, conv_w)


def _pack_bf16_pairs(y):
    half = y.shape[1] // 2
    lo = lax.bitcast_convert_type(y[:, :half].astype(BF16).astype(F32), jnp.uint32)
    hi = lax.bitcast_convert_type(y[:, half:].astype(BF16).astype(F32), jnp.uint32)
    return (lo >> 16) | (hi & jnp.uint32(0xFFFF0000))


def _unpack_bf16_pairs(p):
    lo = lax.bitcast_convert_type(p << 16, F32).astype(BF16)
    hi = lax.bitcast_convert_type(p & jnp.uint32(0xFFFF0000), F32).astype(BF16)
    return lo, hi


def _proj_ln_kernel(a_ref, w_ref, res_ref, g_ref, b_ref, of_ref, ob_ref, *, alpha, nk, row_chunks, packed):
    k = pl.program_id(1)
    rc = a_ref.shape[0] // row_chunks

    def finish(rows, acc):
        y = _layer_norm(alpha * res_ref[rows, :] + acc, g_ref[...], b_ref[...])
        of_ref[rows, :] = y
        ob_ref[rows, :] = _pack_bf16_pairs(y) if packed else y.astype(BF16)

    for c in range(row_chunks):
        rows = pl.ds(c * rc, rc)
        part = _dot(a_ref[rows, :], w_ref[...])
        if nk == 1:
            finish(rows, part)
            continue

        @pl.when(k == 0)
        def _():
            of_ref[rows, :] = part

        @pl.when(jnp.logical_and(k > 0, k < nk - 1))
        def _():
            of_ref[rows, :] += part

        @pl.when(k == nk - 1)
        def _():
            finish(rows, of_ref[rows, :] + part)


def _proj_ln(a, w, res, g, b, *, alpha, name, packed=False):
    n, kdim = a.shape
    d = w.shape[1]
    split = kdim > 2048
    tm = _tile(n, 512)
    tk = _tile(kdim, 1536) if split else kdim
    nk = kdim // tk
    row_chunks = 1 if split else 2
    vmem = 2 * (tm * tk * 2 + tk * d * 2 + tm * d * 4 + tm * d * 6) + 3 * (tm // row_chunks) * d * 4
    ob_shape = jax.ShapeDtypeStruct((n, d // 2), jnp.uint32) if packed else jax.ShapeDtypeStruct((n, d), BF16)
    return pl.pallas_call(
        functools.partial(_proj_ln_kernel, alpha=alpha, nk=nk, row_chunks=row_chunks, packed=packed),
        grid=(n // tm, nk),
        in_specs=[pl.BlockSpec((tm, tk), lambda i, k: (i, k)),
                  pl.BlockSpec((tk, d), lambda i, k: (k, 0)),
                  pl.BlockSpec((tm, d), lambda i, k: (i, 0)),
                  pl.BlockSpec((1, d), lambda i, k: (0, 0)),
                  pl.BlockSpec((1, d), lambda i, k: (0, 0))],
        out_specs=[pl.BlockSpec((tm, d), lambda i, k: (i, 0)),
                   pl.BlockSpec((tm, ob_shape.shape[1]), lambda i, k: (i, 0))],
        out_shape=[jax.ShapeDtypeStruct((n, d), F32), ob_shape],
        compiler_params=_params(("parallel", "arbitrary"), vmem),
        name=name,
    )(a, w, res, g, b)


def _swiglu_up_kernel(x_ref, wg_ref, wu_ref, o_ref, wg16_ref, wu16_ref):
    @pl.when(pl.program_id(1) == 0)
    def _():
        wg16_ref[...] = wg_ref[...].astype(BF16)
        wu16_ref[...] = wu_ref[...].astype(BF16)

    x = x_ref[...]
    o_ref[...] = (jax.nn.silu(_dot(x, wg16_ref[...])) * _dot(x, wu16_ref[...])).astype(o_ref.dtype)


def _swiglu_up(xb, wg, wu, layer):
    n, d = xb.shape
    f = wg.shape[2]
    tm = _tile(n, 1024)
    tf = _tile(f, 512)
    vmem = 2 * (tm * d * 2 + 2 * d * tf * 4 + tm * tf * 2) + 2 * d * tf * 2 + 4 * tm * tf * 4
    w_spec = pl.BlockSpec((None, d, tf), lambda j, i: (layer, 0, j))
    return pl.pallas_call(
        _swiglu_up_kernel,
        grid=(f // tf, n // tm),
        in_specs=[pl.BlockSpec((tm, d), lambda j, i: (i, 0)), w_spec, w_spec],
        out_specs=pl.BlockSpec((tm, tf), lambda j, i: (i, j)),
        out_shape=jax.ShapeDtypeStruct((n, f), BF16),
        scratch_shapes=[pltpu.VMEM((d, tf), BF16), pltpu.VMEM((d, tf), BF16)],
        compiler_params=_params(("arbitrary", "arbitrary"), vmem),
        name="swiglu_up",
    )(xb, wg, wu)


def _attn_kernel(sink_ref, qt_ref, kp_ref, kc_ref, vtp_ref, vtc_ref, o_ref, *, n_heads):
    blk = o_ref.shape[0]
    key = lax.broadcasted_iota(jnp.int32, (blk, blk), 0)
    qry = lax.broadcasted_iota(jnp.int32, (blk, blk), 1)
    from_cur = key <= qry
    prev_bias = jnp.where(pl.program_id(1) > 0, 0.0, NEG_INF)
    heads_per_store = V7X_LANES // HEAD_DIM
    done = []
    for h in range(n_heads):
        qs = slice(h * HEAD_DIM, (h + 1) * HEAD_DIM)
        ks = slice((h // GQA_GROUP) * HEAD_DIM, (h // GQA_GROUP + 1) * HEAD_DIM)
        qt = qt_ref[qs, :]
        s = jnp.where(from_cur, _dot(kc_ref[:, ks], qt), _dot(kp_ref[:, ks], qt) + prev_bias)
        sink = sink_ref[h]
        m = jnp.maximum(jnp.max(s, axis=0, keepdims=True), sink)
        p = jnp.exp(s - m)
        denom = jnp.sum(p, axis=0, keepdims=True) + jnp.exp(sink - m)
        p_c = jnp.where(from_cur, p, 0.0).astype(BF16)
        p_p = jnp.where(from_cur, 0.0, p).astype(BF16)
        ot = _dot(vtc_ref[ks, :], p_c) + _dot(vtp_ref[ks, :], p_p)
        done.append(ot / denom)
        if len(done) == heads_per_store:
            first = h + 1 - heads_per_store
            o_ref[:, first * HEAD_DIM:(h + 1) * HEAD_DIM] = (
                jnp.concatenate(done, axis=0).T.astype(o_ref.dtype))
            done = []


def _attention(qt, k, vt, sinks, *, bsz, seq):
    d, n = qt.shape
    kvd = k.shape[1]
    nb = seq // ATTN_BLOCK
    cur_row = lambda b, i: (b * nb + i, 0)
    prev_row = lambda b, i: (b * nb + jnp.maximum(i - 1, 0), 0)
    cur_col = lambda b, i: (0, b * nb + i)
    prev_col = lambda b, i: (0, b * nb + jnp.maximum(i - 1, 0))
    vmem = 2 * (2 * ATTN_BLOCK * d * 2 + 4 * ATTN_BLOCK * kvd * 2) + 64 * ATTN_BLOCK * ATTN_BLOCK * 4
    return pl.pallas_call(
        functools.partial(_attn_kernel, n_heads=d // HEAD_DIM),
        grid=(bsz, nb),
        in_specs=[pl.BlockSpec(memory_space=pltpu.SMEM),
                  pl.BlockSpec((d, ATTN_BLOCK), cur_col),
                  pl.BlockSpec((ATTN_BLOCK, kvd), prev_row),
                  pl.BlockSpec((ATTN_BLOCK, kvd), cur_row),
                  pl.BlockSpec((kvd, ATTN_BLOCK), prev_col),
                  pl.BlockSpec((kvd, ATTN_BLOCK), cur_col)],
        out_specs=pl.BlockSpec((ATTN_BLOCK, d), cur_row),
        out_shape=jax.ShapeDtypeStruct((n, d), BF16),
        compiler_params=_params(("parallel", "parallel"), vmem),
        name="swa_attention",
    )(sinks, qt, k, k, vt, vt)


def _router_kernel(x_ref, wr_ref, ri_ref, rg_ref, cnt_ref, carry_ref, *, n_experts):
    i = pl.program_id(0)
    tm = x_ref.shape[0]

    @pl.when(i == 0)
    def _():
        carry_ref[...] = jnp.zeros_like(carry_ref)

    x = x_ref[...]
    w = wr_ref[...]
    xh = x.astype(BF16)
    wh = w.astype(BF16)
    xl = (x - xh.astype(F32)).astype(BF16)
    wl = (w - wh.astype(F32)).astype(BF16)
    logits = _dot(xh, wh) + (_dot(xl, wh) + _dot(xh, wl))
    lane = lax.broadcasted_iota(jnp.int32, logits.shape, 1)
    logits = jnp.where(lane < n_experts, logits, -jnp.inf)
    m1 = jnp.max(logits, axis=-1, keepdims=True)
    i1 = jnp.min(jnp.where(logits == m1, lane, V7X_LANES), axis=-1, keepdims=True)
    rest = jnp.where(lane == i1, -jnp.inf, logits)
    m2 = jnp.max(rest, axis=-1, keepdims=True)
    i2 = jnp.min(jnp.where(rest == m2, lane, V7X_LANES), axis=-1, keepdims=True)
    e2 = jnp.exp(m2 - m1)
    g1 = 1.0 / (1.0 + e2)
    g2 = e2 / (1.0 + e2)

    sel1 = lane == i1
    sel2 = lane == i2
    member = jnp.logical_or(sel1, sel2).astype(BF16)
    r = lax.broadcasted_iota(jnp.int32, (tm, tm), 0)
    c = lax.broadcasted_iota(jnp.int32, (tm, tm), 1)
    before = (c < r).astype(BF16)
    prior = carry_ref[...] + _dot(before, member)
    r1 = jnp.sum(jnp.where(sel1, prior, 0.0), axis=-1, keepdims=True).astype(jnp.int32)
    r2 = jnp.sum(jnp.where(sel2, prior, 0.0), axis=-1, keepdims=True).astype(jnp.int32)
    carry_ref[...] += jnp.sum(member.astype(F32), axis=0, keepdims=True)

    ri_ref[...] = jnp.where(lane == 0, i1, jnp.where(lane == 1, i2,
                            jnp.where(lane == 2, r1, jnp.where(lane == 3, r2, 0))))
    rg_ref[...] = jnp.where(lane == 0, g1, jnp.where(lane == 1, g2, 0.0))
    cnt_ref[...] = carry_ref[...].astype(jnp.int32)


def _router(x, w_router_padded, *, n_experts):
    n, d = x.shape
    tm = _tile(n, 512)
    vmem = 2 * (tm * d * 4 + d * V7X_LANES * 4 + 2 * tm * V7X_LANES * 4) + 8 * tm * tm * 4
    return pl.pallas_call(
        functools.partial(_router_kernel, n_experts=n_experts),
        grid=(n // tm,),
        in_specs=[pl.BlockSpec((tm, d), lambda i: (i, 0)),
                  pl.BlockSpec((d, V7X_LANES), lambda i: (0, 0))],
        out_specs=[pl.BlockSpec((tm, V7X_LANES), lambda i: (i, 0)),
                   pl.BlockSpec((tm, V7X_LANES), lambda i: (i, 0)),
                   pl.BlockSpec((1, V7X_LANES), lambda i: (0, 0))],
        out_shape=[jax.ShapeDtypeStruct((n, V7X_LANES), jnp.int32),
                   jax.ShapeDtypeStruct((n, V7X_LANES), F32),
                   jax.ShapeDtypeStruct((1, V7X_LANES), jnp.int32)],
        scratch_shapes=[pltpu.VMEM((1, V7X_LANES), F32)],
        compiler_params=_params(("arbitrary",), vmem),
        name="moe_router",
    )(x, w_router_padded)


def _dispatch_kernel(slot_ref, x_ref, init_hbm, xs_hbm, sem):
    del init_hbm
    chunk = x_ref.shape[0]

    def copies(r):
        return [pltpu.make_async_copy(x_ref.at[pl.ds(r, 1), :],
                                      xs_hbm.at[pl.ds(slot_ref[0, 0, TOP_K * r + k], 1), :], sem)
                for k in range(TOP_K)]

    def start(r, carry):
        for cp in copies(r):
            cp.start()
        return carry

    lax.fori_loop(0, chunk, start, 0, unroll=DMA_ISSUE_UNROLL)
    for _ in range(TOP_K):
        pltpu.make_async_copy(x_ref, xs_hbm.at[pl.ds(0, chunk), :], sem).wait()


def _dispatch(x, slots, *, n_slots):
    n, d = x.shape
    chunk = _tile(n, 512)
    slots3 = slots.reshape(n // chunk, 1, TOP_K * chunk)
    init = jnp.zeros((n_slots, d), x.dtype)
    return pl.pallas_call(
        _dispatch_kernel,
        grid=(n // chunk,),
        in_specs=[pl.BlockSpec((1, 1, TOP_K * chunk), lambda i: (i, 0, 0), memory_space=pltpu.SMEM),
                  pl.BlockSpec((chunk, d), lambda i: (i, 0)),
                  pl.BlockSpec(memory_space=pl.ANY)],
        out_specs=pl.BlockSpec(memory_space=pl.ANY),
        out_shape=jax.ShapeDtypeStruct((n_slots, d), x.dtype),
        scratch_shapes=[pltpu.SemaphoreType.DMA(())],
        input_output_aliases={2: 0},
        compiler_params=_params(("arbitrary",), 2 * chunk * d * 4),
        name="moe_dispatch",
    )(slots3, x, init)


def _moe_up_kernel(te_ref, first_ref, rows_ref, xs_ref, wg_ref, wu_ref, o_ref, wgb_ref, wub_ref):
    del te_ref
    t = pl.program_id(1)
    half = xs_ref.shape[1]

    @pl.when(first_ref[t] == 1)
    def _():
        wgb_ref[...] = wg_ref[...].astype(BF16)
        wub_ref[...] = wu_ref[...].astype(BF16)

    @pl.when(rows_ref[t] > 0)
    def _():
        lo, hi = _unpack_bf16_pairs(xs_ref[...])
        gate = _dot(lo, wgb_ref[0:half, :]) + _dot(hi, wgb_ref[half:2 * half, :])
        up = _dot(lo, wub_ref[0:half, :]) + _dot(hi, wub_ref[half:2 * half, :])
        o_ref[...] = (jax.nn.silu(gate) * up).astype(o_ref.dtype)

    @pl.when(rows_ref[t] == 0)
    def _():
        o_ref[...] = jnp.zeros_like(o_ref)


def _moe_up(xs, w_gate, w_up, layer, tile_expert, tile_first, tile_rows):
    s = xs.shape[0]
    d = 2 * xs.shape[1]
    f = w_gate.shape[3]
    tm = MOE_TILE
    tf = _tile(f, 1024)
    vmem = 2 * (tm * d * 2 + 2 * d * tf * 4 + tm * tf * 2) + 2 * d * tf * 2 + 3 * tm * tf * 4
    w_spec = pl.BlockSpec((None, None, d, tf), lambda j, t, te, fi, nu: (layer, te[t], 0, j))
    grid_spec = pltpu.PrefetchScalarGridSpec(
        num_scalar_prefetch=3,
        grid=(f // tf, s // tm),
        in_specs=[pl.BlockSpec((tm, d // 2), lambda j, t, te, fi, nu: (t, 0)), w_spec, w_spec],
        out_specs=pl.BlockSpec((tm, tf), lambda j, t, te, fi, nu: (t, j)),
        scratch_shapes=[pltpu.VMEM((d, tf), BF16), pltpu.VMEM((d, tf), BF16)],
    )
    return pl.pallas_call(
        _moe_up_kernel,
        grid_spec=grid_spec,
        out_shape=jax.ShapeDtypeStruct((s, f), BF16),
        compiler_params=_params(("arbitrary", "arbitrary"), vmem),
        name="moe_up",
    )(tile_expert, tile_first, tile_rows, xs, w_gate, w_up)


def _moe_down_kernel(te_ref, first_ref, rows_ref, h_ref, wd_ref, o_ref, wdb_ref):
    del te_ref
    t = pl.program_id(1)

    @pl.when(first_ref[t] == 1)
    def _():
        wdb_ref[...] = wd_ref[...].astype(BF16)

    @pl.when(rows_ref[t] > 0)
    def _():
        o_ref[...] = _dot(h_ref[...], wdb_ref[...])

    @pl.when(rows_ref[t] == 0)
    def _():
        o_ref[...] = jnp.zeros_like(o_ref)


def _moe_down(h, w_down, layer, tile_expert, tile_first, tile_rows):
    s, f = h.shape
    d = w_down.shape[3]
    tm = MOE_TILE
    tn = _tile(d, 512)
    vmem = 2 * (tm * f * 2 + tm * tn * 4 + f * tn * 4) + f * tn * 2 + 2 * tm * tn * 4
    grid_spec = pltpu.PrefetchScalarGridSpec(
        num_scalar_prefetch=3,
        grid=(d // tn, s // tm),
        in_specs=[pl.BlockSpec((tm, f), lambda j, t, te, fi, nu: (t, 0)),
                  pl.BlockSpec((None, None, f, tn), lambda j, t, te, fi, nu: (layer, te[t], 0, j))],
        out_specs=pl.BlockSpec((tm, tn), lambda j, t, te, fi, nu: (t, j)),
        scratch_shapes=[pltpu.VMEM((f, tn), BF16)],
    )
    return pl.pallas_call(
        _moe_down_kernel,
        grid_spec=grid_spec,
        out_shape=jax.ShapeDtypeStruct((s, d), F32),
        compiler_params=_params(("arbitrary", "arbitrary"), vmem),
        name="moe_down",
    )(tile_expert, tile_first, tile_rows, h, w_down)


def _combine_ln_kernel(slot_ref, next_slot_ref, y_hbm, rg_ref, res_ref, g_ref, b_ref, of_ref, ob_ref,
                       buf_ref, sem, *, alpha):
    i = pl.program_id(0)
    tm = res_ref.shape[0]

    def gather(slots, parity):
        def start(r, carry):
            for k in range(TOP_K):
                pltpu.make_async_copy(y_hbm.at[pl.ds(slots[0, 0, TOP_K * r + k], 1), :],
                                      buf_ref.at[parity * TOP_K + k, pl.ds(r, 1), :],
                                      sem.at[parity]).start()
            return carry
        lax.fori_loop(0, tm, start, 0, unroll=DMA_ISSUE_UNROLL)

    @pl.when(i == 0)
    def _():
        gather(slot_ref, 0)

    @pl.when(i + 1 < pl.num_programs(0))
    def _():
        gather(next_slot_ref, (i + 1) % 2)

    parity = i % 2
    for k in range(TOP_K):
        pltpu.make_async_copy(y_hbm.at[pl.ds(0, tm), :], buf_ref.at[parity * TOP_K + k],
                              sem.at[parity]).wait()
    gates = rg_ref[...]
    ff = gates[:, 0:1] * buf_ref[parity * TOP_K] + gates[:, 1:2] * buf_ref[parity * TOP_K + 1]
    y = _layer_norm(alpha * res_ref[...] + ff, g_ref[...], b_ref[...])
    of_ref[...] = y
    ob_ref[...] = y.astype(BF16)


def _combine_ln(y, slots, gates, res, g, b, *, alpha):
    n, d = res.shape
    tm = _tile(n, 256)
    steps = n // tm
    slots3 = slots.reshape(steps, 1, TOP_K * tm)
    vmem = (2 * (tm * V7X_LANES * 4 + tm * d * 4 + tm * d * 6) + 2 * TOP_K * tm * d * 4
            + 3 * tm * d * 4)
    slot_block = (1, 1, TOP_K * tm)
    return pl.pallas_call(
        functools.partial(_combine_ln_kernel, alpha=alpha),
        grid=(steps,),
        in_specs=[pl.BlockSpec(slot_block, lambda i: (i, 0, 0), memory_space=pltpu.SMEM),
                  pl.BlockSpec(slot_block, lambda i: (jnp.minimum(i + 1, steps - 1), 0, 0),
                               memory_space=pltpu.SMEM),
                  pl.BlockSpec(memory_space=pl.ANY),
                  pl.BlockSpec((tm, V7X_LANES), lambda i: (i, 0)),
                  pl.BlockSpec((tm, d), lambda i: (i, 0)),
                  pl.BlockSpec((1, d), lambda i: (0, 0)),
                  pl.BlockSpec((1, d), lambda i: (0, 0))],
        out_specs=[pl.BlockSpec((tm, d), lambda i: (i, 0)),
                   pl.BlockSpec((tm, d), lambda i: (i, 0))],
        out_shape=[jax.ShapeDtypeStruct((n, d), F32), jax.ShapeDtypeStruct((n, d), BF16)],
        scratch_shapes=[pltpu.VMEM((2 * TOP_K, tm, d), F32), pltpu.SemaphoreType.DMA((2,))],
        compiler_params=_params(("arbitrary",), vmem),
        name="moe_combine_ln",
    )(slots3, slots3, y, gates, res, g, b)


def _moe_layer(x, x_packed, w_router, w_gate, w_up, w_down, layer, g, b, *, alpha):
    n, d = x.shape
    n_experts = w_router.shape[1]
    tm = MOE_TILE
    n_tiles = (TOP_K * n) // tm + n_experts
    wr = jnp.pad(w_router, ((0, 0), (0, V7X_LANES - n_experts)))
    ri, rg, cnt = _router(x, wr, n_experts=n_experts)

    counts = cnt[0, :n_experts]
    tiles_per = (counts + tm - 1) // tm
    tile_end = jnp.cumsum(tiles_per)
    n_used = tile_end[-1]
    group_start = (tile_end - tiles_per) * tm
    tile_id = jnp.arange(n_tiles, dtype=jnp.int32)
    last_used = jnp.minimum(tile_id, jnp.maximum(n_used - 1, 0))
    tile_expert = jnp.sum((last_used[:, None] >= tile_end[None, :]).astype(jnp.int32), axis=1)
    tile_expert = jnp.minimum(tile_expert, n_experts - 1)
    tile_first = jnp.concatenate([jnp.ones((1,), jnp.int32),
                                  (tile_expert[1:] != tile_expert[:-1]).astype(jnp.int32)])
    slots = (group_start[ri[:, :TOP_K]] + ri[:, TOP_K:2 * TOP_K]).astype(jnp.int32).reshape(-1)
    group_end = group_start + counts
    tile_rows = jnp.clip(group_end[tile_expert] - tile_id * tm, 0, tm)
    tile_rows = jnp.where(tile_id < n_used, tile_rows, 0).astype(jnp.int32)

    xs = _dispatch(x_packed, slots, n_slots=n_tiles * tm)
    h = _moe_up(xs, w_gate, w_up, layer, tile_expert, tile_first, tile_rows)
    y = _moe_down(h, w_down, layer, tile_expert, tile_first, tile_rows)
    return _combine_ln(y, slots, rg, x, g, b, alpha=alpha)


def kernel(x, conv_w_in, conv_w, conv_w_out, w_kv, attn_w_q, attn_sinks, attn_w_o, ln_g, ln_b,
           ffn_w_gate, ffn_w_up, ffn_w_down, moe_w_router, moe_w_gate, moe_w_up, moe_w_down):
    bsz, seq, d = x.shape
    n = bsz * seq
    depth = ln_g.shape[0]
    n_a = conv_w_in.shape[0]
    alpha = float((2 * depth) ** 0.25)
    assert seq % ATTN_BLOCK == 0 and d % (HEAD_DIM * GQA_GROUP) == 0
    assert w_kv.shape[1] == 2 * d // GQA_GROUP

    xf = x.reshape(n, d)
    xb = xf.astype(BF16)
    k_shared = vt_shared = None
    for l in range(depth):
        g0, b0 = ln_g[l, 0].reshape(1, d), ln_b[l, 0].reshape(1, d)
        g1, b1 = ln_g[l, 1].reshape(1, d), ln_b[l, 1].reshape(1, d)
        moe = l % 2 == 1
        if l < n_a:
            gated = _conv_gate(xb, conv_w_in, conv_w, l, seq=seq)
            xf, xb = _proj_ln(gated, conv_w_out[l].astype(BF16), xf, g0, b0, alpha=alpha,
                              name="conv_out_ln", packed=moe)
        else:
            j = l - n_a
            qt = _matmul_nt(attn_w_q[j].T.astype(BF16), xb, out_dtype=BF16,
                            scale=1.0 / math.sqrt(HEAD_DIM), name="attn_q")
            o = _attention(qt, k_shared, vt_shared, attn_sinks[j], bsz=bsz, seq=seq)
            xf, xb = _proj_ln(o, attn_w_o[j].astype(BF16), xf, g0, b0, alpha=alpha,
                              name="attn_out_ln", packed=moe)
        i = l // 2
        if moe:
            xf, xb = _moe_layer(xf, xb, moe_w_router[i], moe_w_gate, moe_w_up, moe_w_down, i,
                                g1, b1, alpha=alpha)
        else:
            h = _swiglu_up(xb, ffn_w_gate, ffn_w_up, i)
            xf, xb = _proj_ln(h, ffn_w_down[i].astype(BF16), xf, g1, b1, alpha=alpha, name="ffn_down_ln")
        if l == n_a - 1:
            kvd = w_kv.shape[1] // 2
            k_shared = _matmul(xb, w_kv[:, :kvd].astype(BF16), out_dtype=BF16, name="k_proj")
            vt_shared = _matmul_nt(w_kv[:, kvd:].T.astype(BF16), xb, out_dtype=BF16, name="v_proj")
    return xf.reshape(bsz, seq, d)
```

```python
import functools
import math

import jax
import jax.numpy as jnp
from jax import lax
from jax.experimental import pallas as pl
from jax.experimental.pallas import tpu as pltpu

F32 = jnp.float32
BF16 = jnp.bfloat16

HEAD_DIM = 64
GQA_GROUP = 8
ATTN_BLOCK = 128
TOP_K = 2
LN_EPS = 1e-5
NEG_INF = -1e30

V7X_LANES = 128
V7X_SUBLANES = 8
V7X_VMEM_BYTES = 64 * 1024 * 1024
V7X_VMEM_BUDGET = V7X_VMEM_BYTES - 8 * 1024 * 1024

MOE_TILE = 512
DMA_ISSUE_UNROLL = 8


def _tile(dim, target, quantum=V7X_LANES):
    if dim <= target:
        return dim
    t = (target // quantum) * quantum
    while t > quantum and dim % t:
        t -= quantum
    assert dim % t == 0, (dim, target)
    return t


def _params(semantics, vmem_bytes):
    limit = min(V7X_VMEM_BUDGET, max(32 * 1024 * 1024, int(vmem_bytes)))
    return pltpu.CompilerParams(dimension_semantics=semantics, vmem_limit_bytes=limit)


def _dot(a, b):
    return jnp.dot(a, b, preferred_element_type=F32)


def _layer_norm(z, g, b):
    mu = jnp.mean(z, axis=-1, keepdims=True)
    zc = z - mu
    var = jnp.mean(zc * zc, axis=-1, keepdims=True)
    return zc * lax.rsqrt(var + LN_EPS) * g + b


def _matmul_kernel(x_ref, w_ref, o_ref, *, scale):
    acc = _dot(x_ref[...], w_ref[...])
    if scale != 1.0:
        acc = acc * scale
    o_ref[...] = acc.astype(o_ref.dtype)


def _matmul(x, w, *, out_dtype, scale=1.0, name):
    n, k = x.shape
    m = w.shape[1]
    tm = _tile(n, 1024)
    tn = _tile(m, 1024)
    vmem = 2 * (tm * k * 2 + k * tn * 2 + tm * tn * 4) + tm * tn * 4
    return pl.pallas_call(
        functools.partial(_matmul_kernel, scale=scale),
        grid=(m // tn, n // tm),
        in_specs=[pl.BlockSpec((tm, k), lambda j, i: (i, 0)),
                  pl.BlockSpec((k, tn), lambda j, i: (0, j))],
        out_specs=pl.BlockSpec((tm, tn), lambda j, i: (i, j)),
        out_shape=jax.ShapeDtypeStruct((n, m), out_dtype),
        compiler_params=_params(("parallel", "parallel"), vmem),
        name=name,
    )(x, w)


def _matmul_nt_kernel(wt_ref, x_ref, o_ref, *, scale):
    acc = lax.dot_general(wt_ref[...], x_ref[...], (((1,), (1,)), ((), ())),
                          preferred_element_type=F32)
    if scale != 1.0:
        acc = acc * scale
    o_ref[...] = acc.astype(o_ref.dtype)


def _matmul_nt(wt, x, *, out_dtype, scale=1.0, name):
    m, k = wt.shape
    n = x.shape[0]
    tm = _tile(n, 1024)
    tn = _tile(m, 1024)
    vmem = 2 * (tm * k * 2 + k * tn * 2 + tm * tn * 4) + tm * tn * 4
    return pl.pallas_call(
        functools.partial(_matmul_nt_kernel, scale=scale),
        grid=(m // tn, n // tm),
        in_specs=[pl.BlockSpec((tn, k), lambda j, i: (j, 0)),
                  pl.BlockSpec((tm, k), lambda j, i: (i, 0))],
        out_specs=pl.BlockSpec((tn, tm), lambda j, i: (j, i)),
        out_shape=jax.ShapeDtypeStruct((m, n), out_dtype),
        compiler_params=_params(("parallel", "parallel"), vmem),
        name=name,
    )(wt, x)


def _conv_gate_kernel(x_ref, wb_ref, wc_ref, wh_ref, cw_ref, o_ref, w16_ref, ubuf_ref, *, tiles_per_seq):
    i = pl.program_id(1)
    tm = x_ref.shape[0]
    pad = V7X_SUBLANES

    @pl.when(i == 0)
    def _():
        w16_ref[0] = wb_ref[...].astype(BF16)
        w16_ref[1] = wc_ref[...].astype(BF16)
        w16_ref[2] = wh_ref[...].astype(BF16)

    @pl.when(i % tiles_per_seq == 0)
    def _():
        ubuf_ref[0:pad, :] = jnp.zeros((pad, ubuf_ref.shape[1]), F32)

    x = x_ref[...]
    u = _dot(x, w16_ref[1]) * _dot(x, w16_ref[2])
    ubuf_ref[pad:pad + tm, :] = u
    u1 = ubuf_ref[pad - 1:pad - 1 + tm, :]
    u2 = ubuf_ref[pad - 2:pad - 2 + tm, :]
    cw = cw_ref[...]
    conv = cw[0:1, :] * u2 + cw[1:2, :] * u1 + cw[2:3, :] * u
    o_ref[...] = (_dot(x, w16_ref[0]) * conv).astype(o_ref.dtype)
    ubuf_ref[0:pad, :] = ubuf_ref[tm:tm + pad, :]


def _conv_gate(xb, w_in, conv_w, layer, *, seq):
    n, d = xb.shape
    tm = _tile(seq, 1024)
    tn = _tile(d, 256)
    nj = d // tn
    vmem = (2 * (tm * d * 2 + 3 * d * tn * 4 + tm * tn * 2) + 3 * d * tn * 2
            + (tm + 8) * tn * 4 + 6 * tm * tn * 4)
    w_spec = lambda third: pl.BlockSpec((None, d, tn), lambda j, i: (layer, 0, third * nj + j))
    return pl.pallas_call(
        functools.partial(_conv_gate_kernel, tiles_per_seq=seq // tm),
        grid=(nj, n // tm),
        in_specs=[pl.BlockSpec((tm, d), lambda j, i: (i, 0)),
                  w_spec(0), w_spec(1), w_spec(2),
                  pl.BlockSpec((None, 3, tn), lambda j, i: (layer, 0, j))],
        out_specs=pl.BlockSpec((tm, tn), lambda j, i: (i, j)),
        out_shape=jax.ShapeDtypeStruct((n, d), BF16),
        scratch_shapes=[pltpu.VMEM((3, d, tn), BF16), pltpu.VMEM((tm + V7X_SUBLANES, tn), F32)],
        compiler_params=_params(("arbitrary", "arbitrary"), vmem),
        name="conv_gate",
    )(xb, w_in, w_in, w_in, conv_w)


def _pack_bf16_pairs(y):
    half = y.shape[1] // 2
    lo = lax.bitcast_convert_type(y[:, :half].astype(BF16).astype(F32), jnp.uint32)
    hi = lax.bitcast_convert_type(y[:, half:].astype(BF16).astype(F32), jnp.uint32)
    return (lo >> 16) | (hi & jnp.uint32(0xFFFF0000))


def _unpack_bf16_pairs(p):
    lo = lax.bitcast_convert_type(p << 16, F32).astype(BF16)
    hi = lax.bitcast_convert_type(p & jnp.uint32(0xFFFF0000), F32).astype(BF16)
    return lo, hi


def _proj_ln_kernel(a_ref, w_ref, res_ref, g_ref, b_ref, of_ref, ob_ref, *, alpha, nk, row_chunks, packed):
    k = pl.program_id(1)
    rc = a_ref.shape[0] // row_chunks

    def finish(rows, acc):
        y = _layer_norm(alpha * res_ref[rows, :] + acc, g_ref[...], b_ref[...])
        of_ref[rows, :] = y
        ob_ref[rows, :] = _pack_bf16_pairs(y) if packed else y.astype(BF16)

    if nk == 1:
        for c in range(row_chunks):
            rows = pl.ds(c * rc, rc)
            finish(rows, _dot(a_ref[rows, :], w_ref[...]))
        return

    @pl.when(k == 0)
    def _():
        of_ref[...] = jnp.zeros_like(of_ref)

    of_ref[...] += _dot(a_ref[...], w_ref[...])

    @pl.when(k == nk - 1)
    def _():
        for c in range(row_chunks):
            rows = pl.ds(c * rc, rc)
            finish(rows, of_ref[rows, :])


def _proj_ln(a, w, res, g, b, *, alpha, name, packed=False):
    n, kdim = a.shape
    d = w.shape[1]
    split = kdim > 2048
    tm = _tile(n, 512)
    tk = _tile(kdim, 1536) if split else kdim
    nk = kdim // tk
    row_chunks = 2
    vmem = (2 * (tm * tk * 2 + tk * d * 2 + tm * d * 4 + tm * d * 6)
            + (tm * d * 4 if split else 0) + 3 * (tm // row_chunks) * d * 4)
    ob_shape = jax.ShapeDtypeStruct((n, d // 2), jnp.uint32) if packed else jax.ShapeDtypeStruct((n, d), BF16)
    return pl.pallas_call(
        functools.partial(_proj_ln_kernel, alpha=alpha, nk=nk, row_chunks=row_chunks, packed=packed),
        grid=(n // tm, nk),
        in_specs=[pl.BlockSpec((tm, tk), lambda i, k: (i, k)),
                  pl.BlockSpec((tk, d), lambda i, k: (k, 0)),
                  pl.BlockSpec((tm, d), lambda i, k: (i, 0)),
                  pl.BlockSpec((1, d), lambda i, k: (0, 0)),
                  pl.BlockSpec((1, d), lambda i, k: (0, 0))],
        out_specs=[pl.BlockSpec((tm, d), lambda i, k: (i, 0)),
                   pl.BlockSpec((tm, ob_shape.shape[1]), lambda i, k: (i, 0))],
        out_shape=[jax.ShapeDtypeStruct((n, d), F32), ob_shape],
        compiler_params=_params(("parallel", "arbitrary"), vmem),
        name=name,
    )(a, w, res, g, b)


def _swiglu_up_kernel(x_ref, wg_ref, wu_ref, o_ref, wg16_ref, wu16_ref):
    @pl.when(pl.program_id(1) == 0)
    def _():
        wg16_ref[...] = wg_ref[...].astype(BF16)
        wu16_ref[...] = wu_ref[...].astype(BF16)

    x = x_ref[...]
    o_ref[...] = (jax.nn.silu(_dot(x, wg16_ref[...])) * _dot(x, wu16_ref[...])).astype(o_ref.dtype)


def _swiglu_up(xb, wg, wu, layer):
    n, d = xb.shape
    f = wg.shape[2]
    tm = _tile(n, 1024)
    tf = _tile(f, 512)
    vmem = 2 * (tm * d * 2 + 2 * d * tf * 4 + tm * tf * 2) + 2 * d * tf * 2 + 4 * tm * tf * 4
    w_spec = pl.BlockSpec((None, d, tf), lambda j, i: (layer, 0, j))
    return pl.pallas_call(
        _swiglu_up_kernel,
        grid=(f // tf, n // tm),
        in_specs=[pl.BlockSpec((tm, d), lambda j, i: (i, 0)), w_spec, w_spec],
        out_specs=pl.BlockSpec((tm, tf), lambda j, i: (i, j)),
        out_shape=jax.ShapeDtypeStruct((n, f), BF16),
        scratch_shapes=[pltpu.VMEM((d, tf), BF16), pltpu.VMEM((d, tf), BF16)],
        compiler_params=_params(("arbitrary", "arbitrary"), vmem),
        name="swiglu_up",
    )(xb, wg, wu)


def _attn_kernel(sink_ref, qt_ref, kp_ref, kc_ref, vtp_ref, vtc_ref, o_ref, *, n_heads):
    blk = o_ref.shape[0]
    key = lax.broadcasted_iota(jnp.int32, (blk, blk), 0)
    qry = lax.broadcasted_iota(jnp.int32, (blk, blk), 1)
    from_cur = key <= qry
    prev_bias = jnp.where(pl.program_id(1) > 0, 0.0, NEG_INF)
    heads_per_store = V7X_LANES // HEAD_DIM
    done = []
    for h in range(n_heads):
        qs = slice(h * HEAD_DIM, (h + 1) * HEAD_DIM)
        ks = slice((h // GQA_GROUP) * HEAD_DIM, (h // GQA_GROUP + 1) * HEAD_DIM)
        qt = qt_ref[qs, :]
        s = jnp.where(from_cur, _dot(kc_ref[:, ks], qt), _dot(kp_ref[:, ks], qt) + prev_bias)
        sink = sink_ref[h]
        m = jnp.maximum(jnp.max(s, axis=0, keepdims=True), sink)
        p = jnp.exp(s - m)
        denom = jnp.sum(p, axis=0, keepdims=True) + jnp.exp(sink - m)
        p_c = jnp.where(from_cur, p, 0.0).astype(BF16)
        p_p = jnp.where(from_cur, 0.0, p).astype(BF16)
        ot = _dot(vtc_ref[ks, :], p_c) + _dot(vtp_ref[ks, :], p_p)
        done.append(ot / denom)
        if len(done) == heads_per_store:
            first = h + 1 - heads_per_store
            o_ref[:, first * HEAD_DIM:(h + 1) * HEAD_DIM] = (
                jnp.concatenate(done, axis=0).T.astype(o_ref.dtype))
            done = []


def _attention(qt, k, vt, sinks, *, bsz, seq):
    d, n = qt.shape
    kvd = k.shape[1]
    nb = seq // ATTN_BLOCK
    cur_row = lambda b, i: (b * nb + i, 0)
    prev_row = lambda b, i: (b * nb + jnp.maximum(i - 1, 0), 0)
    cur_col = lambda b, i: (0, b * nb + i)
    prev_col = lambda b, i: (0, b * nb + jnp.maximum(i - 1, 0))
    vmem = 2 * (2 * ATTN_BLOCK * d * 2 + 4 * ATTN_BLOCK * kvd * 2) + 64 * ATTN_BLOCK * ATTN_BLOCK * 4
    return pl.pallas_call(
        functools.partial(_attn_kernel, n_heads=d // HEAD_DIM),
        grid=(bsz, nb),
        in_specs=[pl.BlockSpec(memory_space=pltpu.SMEM),
                  pl.BlockSpec((d, ATTN_BLOCK), cur_col),
                  pl.BlockSpec((ATTN_BLOCK, kvd), prev_row),
                  pl.BlockSpec((ATTN_BLOCK, kvd), cur_row),
                  pl.BlockSpec((kvd, ATTN_BLOCK), prev_col),
                  pl.BlockSpec((kvd, ATTN_BLOCK), cur_col)],
        out_specs=pl.BlockSpec((ATTN_BLOCK, d), cur_row),
        out_shape=jax.ShapeDtypeStruct((n, d), BF16),
        compiler_params=_params(("parallel", "parallel"), vmem),
        name="swa_attention",
    )(sinks, qt, k, k, vt, vt)


def _router_kernel(x_ref, wr_ref, ri_ref, rg_ref, cnt_ref, carry_ref, *, n_experts):
    i = pl.program_id(0)
    tm = x_ref.shape[0]

    @pl.when(i == 0)
    def _():
        carry_ref[...] = jnp.zeros_like(carry_ref)

    x = x_ref[...]
    w = wr_ref[...]
    xh = x.astype(BF16)
    wh = w.astype(BF16)
    xl = (x - xh.astype(F32)).astype(BF16)
    wl = (w - wh.astype(F32)).astype(BF16)
    logits = _dot(xh, wh) + (_dot(xl, wh) + _dot(xh, wl))
    lane = lax.broadcasted_iota(jnp.int32, logits.shape, 1)
    logits = jnp.where(lane < n_experts, logits, -jnp.inf)
    m1 = jnp.max(logits, axis=-1, keepdims=True)
    i1 = jnp.min(jnp.where(logits == m1, lane, V7X_LANES), axis=-1, keepdims=True)
    rest = jnp.where(lane == i1, -jnp.inf, logits)
    m2 = jnp.max(rest, axis=-1, keepdims=True)
    i2 = jnp.min(jnp.where(rest == m2, lane, V7X_LANES), axis=-1, keepdims=True)
    e2 = jnp.exp(m2 - m1)
    g1 = 1.0 / (1.0 + e2)
    g2 = e2 / (1.0 + e2)

    sel1 = lane == i1
    sel2 = lane == i2
    member = jnp.logical_or(sel1, sel2).astype(BF16)
    r = lax.broadcasted_iota(jnp.int32, (tm, tm), 0)
    c = lax.broadcasted_iota(jnp.int32, (tm, tm), 1)
    before = (c < r).astype(BF16)
    prior = carry_ref[...] + _dot(before, member)
    r1 = jnp.sum(jnp.where(sel1, prior, 0.0), axis=-1, keepdims=True).astype(jnp.int32)
    r2 = jnp.sum(jnp.where(sel2, prior, 0.0), axis=-1, keepdims=True).astype(jnp.int32)
    carry_ref[...] += jnp.sum(member.astype(F32), axis=0, keepdims=True)

    ri_ref[...] = jnp.where(lane == 0, i1, jnp.where(lane == 1, i2,
                            jnp.where(lane == 2, r1, jnp.where(lane == 3, r2, 0))))
    rg_ref[...] = jnp.where(lane == 0, g1, jnp.where(lane == 1, g2, 0.0))
    cnt_ref[...] = carry_ref[...].astype(jnp.int32)


def _router(x, w_router_padded, *, n_experts):
    n, d = x.shape
    tm = _tile(n, 512)
    vmem = 2 * (tm * d * 4 + d * V7X_LANES * 4 + 2 * tm * V7X_LANES * 4) + 8 * tm * tm * 4
    return pl.pallas_call(
        functools.partial(_router_kernel, n_experts=n_experts),
        grid=(n // tm,),
        in_specs=[pl.BlockSpec((tm, d), lambda i: (i, 0)),
                  pl.BlockSpec((d, V7X_LANES), lambda i: (0, 0))],
        out_specs=[pl.BlockSpec((tm, V7X_LANES), lambda i: (i, 0)),
                   pl.BlockSpec((tm, V7X_LANES), lambda i: (i, 0)),
                   pl.BlockSpec((1, V7X_LANES), lambda i: (0, 0))],
        out_shape=[jax.ShapeDtypeStruct((n, V7X_LANES), jnp.int32),
                   jax.ShapeDtypeStruct((n, V7X_LANES), F32),
                   jax.ShapeDtypeStruct((1, V7X_LANES), jnp.int32)],
        scratch_shapes=[pltpu.VMEM((1, V7X_LANES), F32)],
        compiler_params=_params(("arbitrary",), vmem),
        name="moe_router",
    )(x, w_router_padded)


def _dispatch_kernel(slot_ref, x_ref, init_hbm, xs_hbm, sem):
    del init_hbm
    chunk = x_ref.shape[0]

    def copies(r):
        return [pltpu.make_async_copy(x_ref.at[pl.ds(r, 1), :],
                                      xs_hbm.at[pl.ds(slot_ref[0, 0, TOP_K * r + k], 1), :], sem)
                for k in range(TOP_K)]

    def start(r, carry):
        for cp in copies(r):
            cp.start()
        return carry

    lax.fori_loop(0, chunk, start, 0, unroll=DMA_ISSUE_UNROLL)
    for _ in range(TOP_K):
        pltpu.make_async_copy(x_ref, xs_hbm.at[pl.ds(0, chunk), :], sem).wait()


def _dispatch(x, slots, *, n_slots):
    n, d = x.shape
    chunk = _tile(n, 512)
    slots3 = slots.reshape(n // chunk, 1, TOP_K * chunk)
    init = jnp.zeros((n_slots, d), x.dtype)
    return pl.pallas_call(
        _dispatch_kernel,
        grid=(n // chunk,),
        in_specs=[pl.BlockSpec((1, 1, TOP_K * chunk), lambda i: (i, 0, 0), memory_space=pltpu.SMEM),
                  pl.BlockSpec((chunk, d), lambda i: (i, 0)),
                  pl.BlockSpec(memory_space=pl.ANY)],
        out_specs=pl.BlockSpec(memory_space=pl.ANY),
        out_shape=jax.ShapeDtypeStruct((n_slots, d), x.dtype),
        scratch_shapes=[pltpu.SemaphoreType.DMA(())],
        input_output_aliases={2: 0},
        compiler_params=_params(("arbitrary",), 2 * chunk * d * 4),
        name="moe_dispatch",
    )(slots3, x, init)


def _moe_up_kernel(te_ref, rows_ref, xs_ref, wg_ref, wu_ref, o_ref):
    del te_ref
    t = pl.program_id(1)
    half = xs_ref.shape[1]

    @pl.when(rows_ref[t] > 0)
    def _():
        lo, hi = _unpack_bf16_pairs(xs_ref[...])
        gate = _dot(lo, wg_ref[0:half, :].astype(BF16)) + _dot(hi, wg_ref[half:2 * half, :].astype(BF16))
        up = _dot(lo, wu_ref[0:half, :].astype(BF16)) + _dot(hi, wu_ref[half:2 * half, :].astype(BF16))
        o_ref[...] = (jax.nn.silu(gate) * up).astype(o_ref.dtype)

    @pl.when(rows_ref[t] == 0)
    def _():
        o_ref[...] = jnp.zeros_like(o_ref)


def _moe_up(xs, w_gate, w_up, layer, tile_expert, tile_rows):
    s = xs.shape[0]
    d = 2 * xs.shape[1]
    f = w_gate.shape[3]
    tm = MOE_TILE
    tf = _tile(f, 1024)
    vmem = 2 * (tm * d * 2 + 2 * d * tf * 4 + tm * tf * 2) + 4 * tm * tf * 4
    w_spec = pl.BlockSpec((None, None, d, tf), lambda j, t, te, rows: (layer, te[t], 0, j))
    grid_spec = pltpu.PrefetchScalarGridSpec(
        num_scalar_prefetch=2,
        grid=(f // tf, s // tm),
        in_specs=[pl.BlockSpec((tm, d // 2), lambda j, t, te, rows: (t, 0)), w_spec, w_spec],
        out_specs=pl.BlockSpec((tm, tf), lambda j, t, te, rows: (t, j)),
    )
    return pl.pallas_call(
        _moe_up_kernel,
        grid_spec=grid_spec,
        out_shape=jax.ShapeDtypeStruct((s, f), BF16),
        compiler_params=_params(("arbitrary", "arbitrary"), vmem),
        name="moe_up",
    )(tile_expert, tile_rows, xs, w_gate, w_up)


def _moe_down_kernel(te_ref, rows_ref, h_ref, wd_ref, o_ref):
    del te_ref
    t = pl.program_id(1)

    @pl.when(rows_ref[t] > 0)
    def _():
        o_ref[...] = _dot(h_ref[...], wd_ref[...].astype(BF16))

    @pl.when(rows_ref[t] == 0)
    def _():
        o_ref[...] = jnp.zeros_like(o_ref)


def _moe_down(h, w_down, layer, tile_expert, tile_rows):
    s, f = h.shape
    d = w_down.shape[3]
    tm = MOE_TILE
    tn = _tile(d, 512)
    vmem = 2 * (tm * f * 2 + tm * tn * 4 + f * tn * 4) + 2 * tm * tn * 4
    grid_spec = pltpu.PrefetchScalarGridSpec(
        num_scalar_prefetch=2,
        grid=(d // tn, s // tm),
        in_specs=[pl.BlockSpec((tm, f), lambda j, t, te, rows: (t, 0)),
                  pl.BlockSpec((None, None, f, tn), lambda j, t, te, rows: (layer, te[t], 0, j))],
        out_specs=pl.BlockSpec((tm, tn), lambda j, t, te, rows: (t, j)),
    )
    return pl.pallas_call(
        _moe_down_kernel,
        grid_spec=grid_spec,
        out_shape=jax.ShapeDtypeStruct((s, d), F32),
        compiler_params=_params(("arbitrary", "arbitrary"), vmem),
        name="moe_down",
    )(tile_expert, tile_rows, h, w_down)


def _combine_ln_kernel(slot_ref, next_slot_ref, y_hbm, rg_ref, res_ref, g_ref, b_ref, of_ref, ob_ref,
                       buf_ref, sem, *, alpha):
    i = pl.program_id(0)
    tm = res_ref.shape[0]

    def gather(slots, parity):
        def start(r, carry):
            for k in range(TOP_K):
                pltpu.make_async_copy(y_hbm.at[pl.ds(slots[0, 0, TOP_K * r + k], 1), :],
                                      buf_ref.at[parity * TOP_K + k, pl.ds(r, 1), :],
                                      sem.at[parity]).start()
            return carry
        lax.fori_loop(0, tm, start, 0, unroll=DMA_ISSUE_UNROLL)

    @pl.when(i == 0)
    def _():
        gather(slot_ref, 0)

    @pl.when(i + 1 < pl.num_programs(0))
    def _():
        gather(next_slot_ref, (i + 1) % 2)

    parity = i % 2
    for k in range(TOP_K):
        pltpu.make_async_copy(y_hbm.at[pl.ds(0, tm), :], buf_ref.at[parity * TOP_K + k],
                              sem.at[parity]).wait()
    gates = rg_ref[...]
    ff = gates[:, 0:1] * buf_ref[parity * TOP_K] + gates[:, 1:2] * buf_ref[parity * TOP_K + 1]
    y = _layer_norm(alpha * res_ref[...] + ff, g_ref[...], b_ref[...])
    of_ref[...] = y
    ob_ref[...] = y.astype(BF16)


def _combine_ln(y, slots, gates, res, g, b, *, alpha):
    n, d = res.shape
    tm = _tile(n, 256)
    steps = n // tm
    slots3 = slots.reshape(steps, 1, TOP_K * tm)
    vmem = (2 * (tm * V7X_LANES * 4 + tm * d * 4 + tm * d * 6) + 2 * TOP_K * tm * d * 4
            + 3 * tm * d * 4)
    slot_block = (1, 1, TOP_K * tm)
    return pl.pallas_call(
        functools.partial(_combine_ln_kernel, alpha=alpha),
        grid=(steps,),
        in_specs=[pl.BlockSpec(slot_block, lambda i: (i, 0, 0), memory_space=pltpu.SMEM),
                  pl.BlockSpec(slot_block, lambda i: (jnp.minimum(i + 1, steps - 1), 0, 0),
                               memory_space=pltpu.SMEM),
                  pl.BlockSpec(memory_space=pl.ANY),
                  pl.BlockSpec((tm, V7X_LANES), lambda i: (i, 0)),
                  pl.BlockSpec((tm, d), lambda i: (i, 0)),
                  pl.BlockSpec((1, d), lambda i: (0, 0)),
                  pl.BlockSpec((1, d), lambda i: (0, 0))],
        out_specs=[pl.BlockSpec((tm, d), lambda i: (i, 0)),
                   pl.BlockSpec((tm, d), lambda i: (i, 0))],
        out_shape=[jax.ShapeDtypeStruct((n, d), F32), jax.ShapeDtypeStruct((n, d), BF16)],
        scratch_shapes=[pltpu.VMEM((2 * TOP_K, tm, d), F32), pltpu.SemaphoreType.DMA((2,))],
        compiler_params=_params(("arbitrary",), vmem),
        name="moe_combine_ln",
    )(slots3, slots3, y, gates, res, g, b)


def _moe_layer(x, x_packed, w_router, w_gate, w_up, w_down, layer, g, b, *, alpha):
    n, d = x.shape
    n_experts = w_router.shape[1]
    tm = MOE_TILE
    n_tiles = (TOP_K * n) // tm + n_experts
    wr = jnp.pad(w_router, ((0, 0), (0, V7X_LANES - n_experts)))
    ri, rg, cnt = _router(x, wr, n_experts=n_experts)

    counts = cnt[0, :n_experts]
    tiles_per = (counts + tm - 1) // tm
    tile_end = jnp.cumsum(tiles_per)
    n_used = tile_end[-1]
    group_start = (tile_end - tiles_per) * tm
    tile_id = jnp.arange(n_tiles, dtype=jnp.int32)
    last_used = jnp.minimum(tile_id, jnp.maximum(n_used - 1, 0))
    tile_expert = jnp.sum((last_used[:, None] >= tile_end[None, :]).astype(jnp.int32), axis=1)
    tile_expert = jnp.minimum(tile_expert, n_experts - 1)
    slots = (group_start[ri[:, :TOP_K]] + ri[:, TOP_K:2 * TOP_K]).astype(jnp.int32).reshape(-1)
    group_end = group_start + counts
    tile_rows = jnp.clip(group_end[tile_expert] - tile_id * tm, 0, tm)
    tile_rows = jnp.where(tile_id < n_used, tile_rows, 0).astype(jnp.int32)

    xs = _dispatch(x_packed, slots, n_slots=n_tiles * tm)
    h = _moe_up(xs, w_gate, w_up, layer, tile_expert, tile_rows)
    y = _moe_down(h, w_down, layer, tile_expert, tile_rows)
    return _combine_ln(y, slots, rg, x, g, b, alpha=alpha)


def kernel(x, conv_w_in, conv_w, conv_w_out, w_kv, attn_w_q, attn_sinks, attn_w_o, ln_g, ln_b,
           ffn_w_gate, ffn_w_up, ffn_w_down, moe_w_router, moe_w_gate, moe_w_up, moe_w_down):
    bsz, seq, d = x.shape
    n = bsz * seq
    depth = ln_g.shape[0]
    n_a = conv_w_in.shape[0]
    alpha = float((2 * depth) ** 0.25)
    assert seq % ATTN_BLOCK == 0 and d % (HEAD_DIM * GQA_GROUP) == 0
    assert w_kv.shape[1] == 2 * d // GQA_GROUP

    xf = x.reshape(n, d)
    xb = xf.astype(BF16)
    k_shared = vt_shared = None
    for l in range(depth):
        g0, b0 = ln_g[l, 0].reshape(1, d), ln_b[l, 0].reshape(1, d)
        g1, b1 = ln_g[l, 1].reshape(1, d), ln_b[l, 1].reshape(1, d)
        moe = l % 2 == 1
        if l < n_a:
            gated = _conv_gate(xb, conv_w_in, conv_w, l, seq=seq)
            xf, xb = _proj_ln(gated, conv_w_out[l].astype(BF16), xf, g0, b0, alpha=alpha,
                              name="conv_out_ln", packed=moe)
        else:
            j = l - n_a
            qt = _matmul_nt(attn_w_q[j].T.astype(BF16), xb, out_dtype=BF16,
                            scale=1.0 / math.sqrt(HEAD_DIM), name="attn_q")
            o = _attention(qt, k_shared, vt_shared, attn_sinks[j], bsz=bsz, seq=seq)
            xf, xb = _proj_ln(o, attn_w_o[j].astype(BF16), xf, g0, b0, alpha=alpha,
                              name="attn_out_ln", packed=moe)
        i = l // 2
        if moe:
            xf, xb = _moe_layer(xf, xb, moe_w_router[i], moe_w_gate, moe_w_up, moe_w_down, i,
                                g1, b1, alpha=alpha)
        else:
            h = _swiglu_up(xb, ffn_w_gate, ffn_w_up, i)
            xf, xb = _proj_ln(h, ffn_w_down[i].astype(BF16), xf, g1, b1, alpha=alpha, name="ffn_down_ln")
        if l == n_a - 1:
            kvd = w_kv.shape[1] // 2
            k_shared = _matmul(xb, w_kv[:, :kvd].astype(BF16), out_dtype=BF16, name="k_proj")
            vt_shared = _matmul_nt(w_kv[:, kvd:].T.astype(BF16), xb, out_dtype=BF16, name="v_proj")
    return xf.reshape(bsz, seq, d)
```

```python
import functools
import math

import jax
import jax.numpy as jnp
from jax import lax
from jax.experimental import pallas as pl
from jax.experimental.pallas import tpu as pltpu

F32 = jnp.float32
BF16 = jnp.bfloat16

HEAD_DIM = 64
GQA_GROUP = 8
ATTN_BLOCK = 128
TOP_K = 2
LN_EPS = 1e-5
NEG_INF = -1e30

V7X_LANES = 128
V7X_SUBLANES = 8
V7X_VMEM_BYTES = 64 * 1024 * 1024
V7X_VMEM_BUDGET = V7X_VMEM_BYTES - 8 * 1024 * 1024

MOE_TILE = 512
DMA_ISSUE_UNROLL = 8


def _tile(dim, target, quantum=V7X_LANES):
    if dim <= target:
        return dim
    t = (target // quantum) * quantum
    while t > quantum and dim % t:
        t -= quantum
    assert dim % t == 0, (dim, target)
    return t


def _params(semantics, vmem_bytes):
    limit = min(V7X_VMEM_BUDGET, max(32 * 1024 * 1024, int(vmem_bytes)))
    return pltpu.CompilerParams(dimension_semantics=semantics, vmem_limit_bytes=limit)


def _dot(a, b):
    return jnp.dot(a, b, preferred_element_type=F32)


def _layer_norm(z, g, b):
    mu = jnp.mean(z, axis=-1, keepdims=True)
    zc = z - mu
    var = jnp.mean(zc * zc, axis=-1, keepdims=True)
    return zc * lax.rsqrt(var + LN_EPS) * g + b


def _matmul_kernel(x_ref, w_ref, o_ref, *, scale):
    acc = _dot(x_ref[...], w_ref[...])
    if scale != 1.0:
        acc = acc * scale
    o_ref[...] = acc.astype(o_ref.dtype)


def _matmul(x, w, *, out_dtype, scale=1.0, name):
    n, k = x.shape
    m = w.shape[1]
    tm = _tile(n, 1024)
    tn = _tile(m, 1024)
    vmem = 2 * (tm * k * 2 + k * tn * 2 + tm * tn * 4) + tm * tn * 4
    return pl.pallas_call(
        functools.partial(_matmul_kernel, scale=scale),
        grid=(m // tn, n // tm),
        in_specs=[pl.BlockSpec((tm, k), lambda j, i: (i, 0)),
                  pl.BlockSpec((k, tn), lambda j, i: (0, j))],
        out_specs=pl.BlockSpec((tm, tn), lambda j, i: (i, j)),
        out_shape=jax.ShapeDtypeStruct((n, m), out_dtype),
        compiler_params=_params(("parallel", "parallel"), vmem),
        name=name,
    )(x, w)


def _matmul_nt_kernel(wt_ref, x_ref, o_ref, *, scale):
    acc = lax.dot_general(wt_ref[...], x_ref[...], (((1,), (1,)), ((), ())),
                          preferred_element_type=F32)
    if scale != 1.0:
        acc = acc * scale
    o_ref[...] = acc.astype(o_ref.dtype)


def _matmul_nt(wt, x, *, out_dtype, scale=1.0, name):
    m, k = wt.shape
    n = x.shape[0]
    tm = _tile(n, 1024)
    tn = _tile(m, 1024)
    vmem = 2 * (tm * k * 2 + k * tn * 2 + tm * tn * 4) + tm * tn * 4
    return pl.pallas_call(
        functools.partial(_matmul_nt_kernel, scale=scale),
        grid=(m // tn, n // tm),
        in_specs=[pl.BlockSpec((tn, k), lambda j, i: (j, 0)),
                  pl.BlockSpec((tm, k), lambda j, i: (i, 0))],
        out_specs=pl.BlockSpec((tn, tm), lambda j, i: (j, i)),
        out_shape=jax.ShapeDtypeStruct((m, n), out_dtype),
        compiler_params=_params(("parallel", "parallel"), vmem),
        name=name,
    )(wt, x)


def _conv_gate_kernel(x_ref, wb_ref, wc_ref, wh_ref, cw_ref, o_ref, ubuf_ref, *, tiles_per_seq):
    i = pl.program_id(1)
    tm = x_ref.shape[0]
    pad = V7X_SUBLANES

    @pl.when(i % tiles_per_seq == 0)
    def _():
        ubuf_ref[0:pad, :] = jnp.zeros((pad, ubuf_ref.shape[1]), F32)

    x = x_ref[...]
    u = _dot(x, wc_ref[...].astype(BF16)) * _dot(x, wh_ref[...].astype(BF16))
    ubuf_ref[pad:pad + tm, :] = u
    u1 = ubuf_ref[pad - 1:pad - 1 + tm, :]
    u2 = ubuf_ref[pad - 2:pad - 2 + tm, :]
    cw = cw_ref[...]
    conv = cw[0:1, :] * u2 + cw[1:2, :] * u1 + cw[2:3, :] * u
    o_ref[...] = (_dot(x, wb_ref[...].astype(BF16)) * conv).astype(o_ref.dtype)
    ubuf_ref[0:pad, :] = ubuf_ref[tm:tm + pad, :]


def _conv_gate(xb, w_in, conv_w, layer, *, seq):
    n, d = xb.shape
    tm = _tile(seq, 1024)
    tn = _tile(d, 512)
    nj = d // tn
    vmem = 2 * (tm * d * 2 + 3 * d * tn * 4 + tm * tn * 2) + (tm + 8) * tn * 4 + 6 * tm * tn * 4
    w_spec = lambda third: pl.BlockSpec((None, d, tn), lambda j, i: (layer, 0, third * nj + j))
    return pl.pallas_call(
        functools.partial(_conv_gate_kernel, tiles_per_seq=seq // tm),
        grid=(nj, n // tm),
        in_specs=[pl.BlockSpec((tm, d), lambda j, i: (i, 0)),
                  w_spec(0), w_spec(1), w_spec(2),
                  pl.BlockSpec((None, 3, tn), lambda j, i: (layer, 0, j))],
        out_specs=pl.BlockSpec((tm, tn), lambda j, i: (i, j)),
        out_shape=jax.ShapeDtypeStruct((n, d), BF16),
        scratch_shapes=[pltpu.VMEM((tm + V7X_SUBLANES, tn), F32)],
        compiler_params=_params(("parallel", "arbitrary"), vmem),
        name="conv_gate",
    )(xb, w_in, w_in, w_in, conv_w)


def _pack_bf16_pairs(y):
    half = y.shape[1] // 2
    lo = lax.bitcast_convert_type(y[:, :half].astype(BF16).astype(F32), jnp.uint32)
    hi = lax.bitcast_convert_type(y[:, half:].astype(BF16).astype(F32), jnp.uint32)
    return (lo >> 16) | (hi & jnp.uint32(0xFFFF0000))


def _unpack_bf16_pairs(p):
    lo = lax.bitcast_convert_type(p << 16, F32).astype(BF16)
    hi = lax.bitcast_convert_type(p & jnp.uint32(0xFFFF0000), F32).astype(BF16)
    return lo, hi


def _proj_ln_kernel(a_ref, w_ref, res_ref, g_ref, b_ref, of_ref, ob_ref, *, alpha, nk, row_chunks, packed):
    k = pl.program_id(1)
    rc = a_ref.shape[0] // row_chunks

    def finish(rows, acc):
        y = _layer_norm(alpha * res_ref[rows, :] + acc, g_ref[...], b_ref[...])
        of_ref[rows, :] = y
        ob_ref[rows, :] = _pack_bf16_pairs(y) if packed else y.astype(BF16)

    if nk == 1:
        for c in range(row_chunks):
            rows = pl.ds(c * rc, rc)
            finish(rows, _dot(a_ref[rows, :], w_ref[...]))
        return

    @pl.when(k == 0)
    def _():
        of_ref[...] = jnp.zeros_like(of_ref)

    of_ref[...] += _dot(a_ref[...], w_ref[...])

    @pl.when(k == nk - 1)
    def _():
        for c in range(row_chunks):
            rows = pl.ds(c * rc, rc)
            finish(rows, of_ref[rows, :])


def _proj_ln(a, w, res, g, b, *, alpha, name, packed=False):
    n, kdim = a.shape
    d = w.shape[1]
    split = kdim > 2048
    tm = _tile(n, 512)
    tk = _tile(kdim, 1536) if split else kdim
    nk = kdim // tk
    row_chunks = 2
    vmem = (2 * (tm * tk * 2 + tk * d * 2 + tm * d * 4 + tm * d * 6)
            + (tm * d * 4 if split else 0) + 3 * (tm // row_chunks) * d * 4)
    ob_shape = jax.ShapeDtypeStruct((n, d // 2), jnp.uint32) if packed else jax.ShapeDtypeStruct((n, d), BF16)
    return pl.pallas_call(
        functools.partial(_proj_ln_kernel, alpha=alpha, nk=nk, row_chunks=row_chunks, packed=packed),
        grid=(n // tm, nk),
        in_specs=[pl.BlockSpec((tm, tk), lambda i, k: (i, k)),
                  pl.BlockSpec((tk, d), lambda i, k: (k, 0)),
                  pl.BlockSpec((tm, d), lambda i, k: (i, 0)),
                  pl.BlockSpec((1, d), lambda i, k: (0, 0)),
                  pl.BlockSpec((1, d), lambda i, k: (0, 0))],
        out_specs=[pl.BlockSpec((tm, d), lambda i, k: (i, 0)),
                   pl.BlockSpec((tm, ob_shape.shape[1]), lambda i, k: (i, 0))],
        out_shape=[jax.ShapeDtypeStruct((n, d), F32), ob_shape],
        compiler_params=_params(("parallel", "arbitrary"), vmem),
        name=name,
    )(a, w, res, g, b)


def _swiglu_up_kernel(x_ref, wg_ref, wu_ref, o_ref, wg16_ref, wu16_ref):
    @pl.when(pl.program_id(1) == 0)
    def _():
        wg16_ref[...] = wg_ref[...].astype(BF16)
        wu16_ref[...] = wu_ref[...].astype(BF16)

    x = x_ref[...]
    o_ref[...] = (jax.nn.silu(_dot(x, wg16_ref[...])) * _dot(x, wu16_ref[...])).astype(o_ref.dtype)


def _swiglu_up(xb, wg, wu, layer):
    n, d = xb.shape
    f = wg.shape[2]
    tm = _tile(n, 1024)
    tf = _tile(f, 512)
    vmem = 2 * (tm * d * 2 + 2 * d * tf * 4 + tm * tf * 2) + 2 * d * tf * 2 + 4 * tm * tf * 4
    w_spec = pl.BlockSpec((None, d, tf), lambda j, i: (layer, 0, j))
    return pl.pallas_call(
        _swiglu_up_kernel,
        grid=(f // tf, n // tm),
        in_specs=[pl.BlockSpec((tm, d), lambda j, i: (i, 0)), w_spec, w_spec],
        out_specs=pl.BlockSpec((tm, tf), lambda j, i: (i, j)),
        out_shape=jax.ShapeDtypeStruct((n, f), BF16),
        scratch_shapes=[pltpu.VMEM((d, tf), BF16), pltpu.VMEM((d, tf), BF16)],
        compiler_params=_params(("arbitrary", "arbitrary"), vmem),
        name="swiglu_up",
    )(xb, wg, wu)


def _attn_kernel(sink_ref, qt_ref, kp_ref, kc_ref, vtp_ref, vtc_ref, o_ref, *, n_heads):
    blk = o_ref.shape[0]
    key = lax.broadcasted_iota(jnp.int32, (blk, blk), 0)
    qry = lax.broadcasted_iota(jnp.int32, (blk, blk), 1)
    from_cur = key <= qry
    prev_bias = jnp.where(pl.program_id(1) > 0, 0.0, NEG_INF)
    heads_per_store = V7X_LANES // HEAD_DIM
    done = []
    for h in range(n_heads):
        qs = slice(h * HEAD_DIM, (h + 1) * HEAD_DIM)
        ks = slice((h // GQA_GROUP) * HEAD_DIM, (h // GQA_GROUP + 1) * HEAD_DIM)
        qt = qt_ref[qs, :]
        s = jnp.where(from_cur, _dot(kc_ref[:, ks], qt), _dot(kp_ref[:, ks], qt) + prev_bias)
        sink = sink_ref[h]
        m = jnp.maximum(jnp.max(s, axis=0, keepdims=True), sink)
        p = jnp.exp(s - m)
        denom = jnp.sum(p, axis=0, keepdims=True) + jnp.exp(sink - m)
        p_c = jnp.where(from_cur, p, 0.0).astype(BF16)
        p_p = jnp.where(from_cur, 0.0, p).astype(BF16)
        ot = _dot(vtc_ref[ks, :], p_c) + _dot(vtp_ref[ks, :], p_p)
        done.append(ot / denom)
        if len(done) == heads_per_store:
            first = h + 1 - heads_per_store
            o_ref[:, first * HEAD_DIM:(h + 1) * HEAD_DIM] = (
                jnp.concatenate(done, axis=0).T.astype(o_ref.dtype))
            done = []


def _attention(qt, k, vt, sinks, *, bsz, seq):
    d, n = qt.shape
    kvd = k.shape[1]
    nb = seq // ATTN_BLOCK
    cur_row = lambda b, i: (b * nb + i, 0)
    prev_row = lambda b, i: (b * nb + jnp.maximum(i - 1, 0), 0)
    cur_col = lambda b, i: (0, b * nb + i)
    prev_col = lambda b, i: (0, b * nb + jnp.maximum(i - 1, 0))
    vmem = 2 * (2 * ATTN_BLOCK * d * 2 + 4 * ATTN_BLOCK * kvd * 2) + 64 * ATTN_BLOCK * ATTN_BLOCK * 4
    return pl.pallas_call(
        functools.partial(_attn_kernel, n_heads=d // HEAD_DIM),
        grid=(bsz, nb),
        in_specs=[pl.BlockSpec(memory_space=pltpu.SMEM),
                  pl.BlockSpec((d, ATTN_BLOCK), cur_col),
                  pl.BlockSpec((ATTN_BLOCK, kvd), prev_row),
                  pl.BlockSpec((ATTN_BLOCK, kvd), cur_row),
                  pl.BlockSpec((kvd, ATTN_BLOCK), prev_col),
                  pl.BlockSpec((kvd, ATTN_BLOCK), cur_col)],
        out_specs=pl.BlockSpec((ATTN_BLOCK, d), cur_row),
        out_shape=jax.ShapeDtypeStruct((n, d), BF16),
        compiler_params=_params(("parallel", "parallel"), vmem),
        name="swa_attention",
    )(sinks, qt, k, k, vt, vt)


def _router_kernel(x_ref, wr_ref, ri_ref, rg_ref, cnt_ref, carry_ref, *, n_experts):
    i = pl.program_id(0)
    tm = x_ref.shape[0]

    @pl.when(i == 0)
    def _():
        carry_ref[...] = jnp.zeros_like(carry_ref)

    x = x_ref[...]
    w = wr_ref[...]
    xh = x.astype(BF16)
    wh = w.astype(BF16)
    xl = (x - xh.astype(F32)).astype(BF16)
    wl = (w - wh.astype(F32)).astype(BF16)
    logits = _dot(xh, wh) + (_dot(xl, wh) + _dot(xh, wl))
    lane = lax.broadcasted_iota(jnp.int32, logits.shape, 1)
    logits = jnp.where(lane < n_experts, logits, -jnp.inf)
    m1 = jnp.max(logits, axis=-1, keepdims=True)
    i1 = jnp.min(jnp.where(logits == m1, lane, V7X_LANES), axis=-1, keepdims=True)
    rest = jnp.where(lane == i1, -jnp.inf, logits)
    m2 = jnp.max(rest, axis=-1, keepdims=True)
    i2 = jnp.min(jnp.where(rest == m2, lane, V7X_LANES), axis=-1, keepdims=True)
    e2 = jnp.exp(m2 - m1)
    g1 = 1.0 / (1.0 + e2)
    g2 = e2 / (1.0 + e2)

    sel1 = lane == i1
    sel2 = lane == i2
    member = jnp.logical_or(sel1, sel2).astype(BF16)
    r = lax.broadcasted_iota(jnp.int32, (tm, tm), 0)
    c = lax.broadcasted_iota(jnp.int32, (tm, tm), 1)
    before = (c < r).astype(BF16)
    prior = carry_ref[...] + _dot(before, member)
    r1 = jnp.sum(jnp.where(sel1, prior, 0.0), axis=-1, keepdims=True).astype(jnp.int32)
    r2 = jnp.sum(jnp.where(sel2, prior, 0.0), axis=-1, keepdims=True).astype(jnp.int32)
    carry_ref[...] += jnp.sum(member.astype(F32), axis=0, keepdims=True)

    ri_ref[...] = jnp.where(lane == 0, i1, jnp.where(lane == 1, i2,
                            jnp.where(lane == 2, r1, jnp.where(lane == 3, r2, 0))))
    rg_ref[...] = jnp.where(lane == 0, g1, jnp.where(lane == 1, g2, 0.0))
    cnt_ref[...] = carry_ref[...].astype(jnp.int32)


def _router(x, w_router_padded, *, n_experts):
    n, d = x.shape
    tm = _tile(n, 512)
    vmem = 2 * (tm * d * 4 + d * V7X_LANES * 4 + 2 * tm * V7X_LANES * 4) + 8 * tm * tm * 4
    return pl.pallas_call(
        functools.partial(_router_kernel, n_experts=n_experts),
        grid=(n // tm,),
        in_specs=[pl.BlockSpec((tm, d), lambda i: (i, 0)),
                  pl.BlockSpec((d, V7X_LANES), lambda i: (0, 0))],
        out_specs=[pl.BlockSpec((tm, V7X_LANES), lambda i: (i, 0)),
                   pl.BlockSpec((tm, V7X_LANES), lambda i: (i, 0)),
                   pl.BlockSpec((1, V7X_LANES), lambda i: (0, 0))],
        out_shape=[jax.ShapeDtypeStruct((n, V7X_LANES), jnp.int32),
                   jax.ShapeDtypeStruct((n, V7X_LANES), F32),
                   jax.ShapeDtypeStruct((1, V7X_LANES), jnp.int32)],
        scratch_shapes=[pltpu.VMEM((1, V7X_LANES), F32)],
        compiler_params=_params(("arbitrary",), vmem),
        name="moe_router",
    )(x, w_router_padded)


def _dispatch_kernel(slot_ref, x_ref, init_hbm, xs_hbm, sem):
    del init_hbm
    chunk = x_ref.shape[0]

    def copies(r):
        return [pltpu.make_async_copy(x_ref.at[pl.ds(r, 1), :],
                                      xs_hbm.at[pl.ds(slot_ref[0, 0, TOP_K * r + k], 1), :], sem)
                for k in range(TOP_K)]

    def start(r, carry):
        for cp in copies(r):
            cp.start()
        return carry

    lax.fori_loop(0, chunk, start, 0, unroll=DMA_ISSUE_UNROLL)
    for _ in range(TOP_K):
        pltpu.make_async_copy(x_ref, xs_hbm.at[pl.ds(0, chunk), :], sem).wait()


def _dispatch(x, slots, *, n_slots):
    n, d = x.shape
    chunk = _tile(n, 512)
    slots3 = slots.reshape(n // chunk, 1, TOP_K * chunk)
    init = jnp.zeros((n_slots, d), x.dtype)
    return pl.pallas_call(
        _dispatch_kernel,
        grid=(n // chunk,),
        in_specs=[pl.BlockSpec((1, 1, TOP_K * chunk), lambda i: (i, 0, 0), memory_space=pltpu.SMEM),
                  pl.BlockSpec((chunk, d), lambda i: (i, 0)),
                  pl.BlockSpec(memory_space=pl.ANY)],
        out_specs=pl.BlockSpec(memory_space=pl.ANY),
        out_shape=jax.ShapeDtypeStruct((n_slots, d), x.dtype),
        scratch_shapes=[pltpu.SemaphoreType.DMA(())],
        input_output_aliases={2: 0},
        compiler_params=_params(("arbitrary",), 2 * chunk * d * 4),
        name="moe_dispatch",
    )(slots3, x, init)


def _moe_up_kernel(te_ref, rows_ref, xs_ref, wg_ref, wu_ref, o_ref):
    del te_ref
    t = pl.program_id(1)

    @pl.when(rows_ref[t] > 0)
    def _():
        x = jnp.concatenate(_unpack_bf16_pairs(xs_ref[...]), axis=1)
        gate = _dot(x, wg_ref[...].astype(BF16))
        up = _dot(x, wu_ref[...].astype(BF16))
        o_ref[...] = (jax.nn.silu(gate) * up).astype(o_ref.dtype)

    @pl.when(rows_ref[t] == 0)
    def _():
        o_ref[...] = jnp.zeros_like(o_ref)


def _moe_up(xs, w_gate, w_up, layer, tile_expert, tile_rows):
    s = xs.shape[0]
    d = 2 * xs.shape[1]
    f = w_gate.shape[3]
    tm = MOE_TILE
    tf = _tile(f, 1024)
    vmem = 2 * (tm * d * 2 + 2 * d * tf * 4 + tm * tf * 2) + 4 * tm * tf * 4
    w_spec = pl.BlockSpec((None, None, d, tf), lambda j, t, te, rows: (layer, te[t], 0, j))
    grid_spec = pltpu.PrefetchScalarGridSpec(
        num_scalar_prefetch=2,
        grid=(f // tf, s // tm),
        in_specs=[pl.BlockSpec((tm, d // 2), lambda j, t, te, rows: (t, 0)), w_spec, w_spec],
        out_specs=pl.BlockSpec((tm, tf), lambda j, t, te, rows: (t, j)),
    )
    return pl.pallas_call(
        _moe_up_kernel,
        grid_spec=grid_spec,
        out_shape=jax.ShapeDtypeStruct((s, f), BF16),
        compiler_params=_params(("arbitrary", "arbitrary"), vmem),
        name="moe_up",
    )(tile_expert, tile_rows, xs, w_gate, w_up)


def _moe_down_kernel(te_ref, rows_ref, h_ref, wd_ref, o_ref):
    del te_ref
    t = pl.program_id(1)

    @pl.when(rows_ref[t] > 0)
    def _():
        o_ref[...] = _dot(h_ref[...], wd_ref[...].astype(BF16))

    @pl.when(rows_ref[t] == 0)
    def _():
        o_ref[...] = jnp.zeros_like(o_ref)


def _moe_down(h, w_down, layer, tile_expert, tile_rows):
    s, f = h.shape
    d = w_down.shape[3]
    tm = MOE_TILE
    tn = _tile(d, 512)
    vmem = 2 * (tm * f * 2 + tm * tn * 4 + f * tn * 4) + 2 * tm * tn * 4
    grid_spec = pltpu.PrefetchScalarGridSpec(
        num_scalar_prefetch=2,
        grid=(d // tn, s // tm),
        in_specs=[pl.BlockSpec((tm, f), lambda j, t, te, rows: (t, 0)),
                  pl.BlockSpec((None, None, f, tn), lambda j, t, te, rows: (layer, te[t], 0, j))],
        out_specs=pl.BlockSpec((tm, tn), lambda j, t, te, rows: (t, j)),
    )
    return pl.pallas_call(
        _moe_down_kernel,
        grid_spec=grid_spec,
        out_shape=jax.ShapeDtypeStruct((s, d), F32),
        compiler_params=_params(("arbitrary", "arbitrary"), vmem),
        name="moe_down",
    )(tile_expert, tile_rows, h, w_down)


def _combine_ln_kernel(slot_ref, next_slot_ref, y_hbm, rg_ref, res_ref, g_ref, b_ref, of_ref, ob_ref,
                       buf_ref, sem, *, alpha):
    i = pl.program_id(0)
    tm = res_ref.shape[0]

    def gather(slots, parity):
        def start(r, carry):
            for k in range(TOP_K):
                pltpu.make_async_copy(y_hbm.at[pl.ds(slots[0, 0, TOP_K * r + k], 1), :],
                                      buf_ref.at[parity * TOP_K + k, pl.ds(r, 1), :],
                                      sem.at[parity]).start()
            return carry
        lax.fori_loop(0, tm, start, 0, unroll=DMA_ISSUE_UNROLL)

    @pl.when(i == 0)
    def _():
        gather(slot_ref, 0)

    @pl.when(i + 1 < pl.num_programs(0))
    def _():
        gather(next_slot_ref, (i + 1) % 2)

    parity = i % 2
    for k in range(TOP_K):
        pltpu.make_async_copy(y_hbm.at[pl.ds(0, tm), :], buf_ref.at[parity * TOP_K + k],
                              sem.at[parity]).wait()
    gates = rg_ref[...]
    ff = gates[:, 0:1] * buf_ref[parity * TOP_K] + gates[:, 1:2] * buf_ref[parity * TOP_K + 1]
    y = _layer_norm(alpha * res_ref[...] + ff, g_ref[...], b_ref[...])
    of_ref[...] = y
    ob_ref[...] = y.astype(BF16)


def _combine_ln(y, slots, gates, res, g, b, *, alpha):
    n, d = res.shape
    tm = _tile(n, 256)
    steps = n // tm
    slots3 = slots.reshape(steps, 1, TOP_K * tm)
    vmem = (2 * (tm * V7X_LANES * 4 + tm * d * 4 + tm * d * 6) + 2 * TOP_K * tm * d * 4
            + 3 * tm * d * 4)
    slot_block = (1, 1, TOP_K * tm)
    return pl.pallas_call(
        functools.partial(_combine_ln_kernel, alpha=alpha),
        grid=(steps,),
        in_specs=[pl.BlockSpec(slot_block, lambda i: (i, 0, 0), memory_space=pltpu.SMEM),
                  pl.BlockSpec(slot_block, lambda i: (jnp.minimum(i + 1, steps - 1), 0, 0),
                               memory_space=pltpu.SMEM),
                  pl.BlockSpec(memory_space=pl.ANY),
                  pl.BlockSpec((tm, V7X_LANES), lambda i: (i, 0)),
                  pl.BlockSpec((tm, d), lambda i: (i, 0)),
                  pl.BlockSpec((1, d), lambda i: (0, 0)),
                  pl.BlockSpec((1, d), lambda i: (0, 0))],
        out_specs=[pl.BlockSpec((tm, d), lambda i: (i, 0)),
                   pl.BlockSpec((tm, d), lambda i: (i, 0))],
        out_shape=[jax.ShapeDtypeStruct((n, d), F32), jax.ShapeDtypeStruct((n, d), BF16)],
        scratch_shapes=[pltpu.VMEM((2 * TOP_K, tm, d), F32), pltpu.SemaphoreType.DMA((2,))],
        compiler_params=_params(("arbitrary",), vmem),
        name="moe_combine_ln",
    )(slots3, slots3, y, gates, res, g, b)


def _moe_layer(x, x_packed, w_router, w_gate, w_up, w_down, layer, g, b, *, alpha):
    n, d = x.shape
    n_experts = w_router.shape[1]
    tm = MOE_TILE
    n_tiles = (TOP_K * n) // tm + n_experts
    wr = jnp.pad(w_router, ((0, 0), (0, V7X_LANES - n_experts)))
    ri, rg, cnt = _router(x, wr, n_experts=n_experts)

    counts = cnt[0, :n_experts]
    tiles_per = (counts + tm - 1) // tm
    tile_end = jnp.cumsum(tiles_per)
    n_used = tile_end[-1]
    group_start = (tile_end - tiles_per) * tm
    tile_id = jnp.arange(n_tiles, dtype=jnp.int32)
    last_used = jnp.minimum(tile_id, jnp.maximum(n_used - 1, 0))
    tile_expert = jnp.sum((last_used[:, None] >= tile_end[None, :]).astype(jnp.int32), axis=1)
    tile_expert = jnp.minimum(tile_expert, n_experts - 1)
    slots = (group_start[ri[:, :TOP_K]] + ri[:, TOP_K:2 * TOP_K]).astype(jnp.int32).reshape(-1)
    group_end = group_start + counts
    tile_rows = jnp.clip(group_end[tile_expert] - tile_id * tm, 0, tm)
    tile_rows = jnp.where(tile_id < n_used, tile_rows, 0).astype(jnp.int32)

    xs = _dispatch(x_packed, slots, n_slots=n_tiles * tm)
    h = _moe_up(xs, w_gate, w_up, layer, tile_expert, tile_rows)
    y = _moe_down(h, w_down, layer, tile_expert, tile_rows)
    return _combine_ln(y, slots, rg, x, g, b, alpha=alpha)


def kernel(x, conv_w_in, conv_w, conv_w_out, w_kv, attn_w_q, attn_sinks, attn_w_o, ln_g, ln_b,
           ffn_w_gate, ffn_w_up, ffn_w_down, moe_w_router, moe_w_gate, moe_w_up, moe_w_down):
    bsz, seq, d = x.shape
    n = bsz * seq
    depth = ln_g.shape[0]
    n_a = conv_w_in.shape[0]
    alpha = float((2 * depth) ** 0.25)
    assert seq % ATTN_BLOCK == 0 and d % (HEAD_DIM * GQA_GROUP) == 0
    assert w_kv.shape[1] == 2 * d // GQA_GROUP

    xf = x.reshape(n, d)
    xb = xf.astype(BF16)
    k_shared = vt_shared = None
    for l in range(depth):
        g0, b0 = ln_g[l, 0].reshape(1, d), ln_b[l, 0].reshape(1, d)
        g1, b1 = ln_g[l, 1].reshape(1, d), ln_b[l, 1].reshape(1, d)
        moe = l % 2 == 1
        if l < n_a:
            gated = _conv_gate(xb, conv_w_in, conv_w, l, seq=seq)
            xf, xb = _proj_ln(gated, conv_w_out[l].astype(BF16), xf, g0, b0, alpha=alpha,
                              name="conv_out_ln", packed=moe)
        else:
            j = l - n_a
            qt = _matmul_nt(attn_w_q[j].T.astype(BF16), xb, out_dtype=BF16,
                            scale=1.0 / math.sqrt(HEAD_DIM), name="attn_q")
            o = _attention(qt, k_shared, vt_shared, attn_sinks[j], bsz=bsz, seq=seq)
            xf, xb = _proj_ln(o, attn_w_o[j].astype(BF16), xf, g0, b0, alpha=alpha,
                              name="attn_out_ln", packed=moe)
        i = l // 2
        if moe:
            xf, xb = _moe_layer(xf, xb, moe_w_router[i], moe_w_gate, moe_w_up, moe_w_down, i,
                                g1, b1, alpha=alpha)
        else:
            h = _swiglu_up(xb, ffn_w_gate, ffn_w_up, i)
            xf, xb = _proj_ln(h, ffn_w_down[i].astype(BF16), xf, g1, b1, alpha=alpha, name="ffn_down_ln")
        if l == n_a - 1:
            kvd = w_kv.shape[1] // 2
            k_shared = _matmul(xb, w_kv[:, :kvd].astype(BF16), out_dtype=BF16, name="k_proj")
            vt_shared = _matmul_nt(w_kv[:, kvd:].T.astype(BF16), xb, out_dtype=BF16, name="v_proj")
    return xf.reshape(bsz, seq, d)
```

```python
import functools
import math

import jax
import jax.numpy as jnp
from jax import lax
from jax.experimental import pallas as pl
from jax.experimental.pallas import tpu as pltpu

F32 = jnp.float32
BF16 = jnp.bfloat16

HEAD_DIM = 64
GQA_GROUP = 8
ATTN_BLOCK = 128
TOP_K = 2
LN_EPS = 1e-5
NEG_INF = -1e30

V7X_LANES = 128
V7X_SUBLANES = 8
V7X_VMEM_BYTES = 64 * 1024 * 1024
V7X_VMEM_BUDGET = V7X_VMEM_BYTES - 8 * 1024 * 1024

MOE_TILE = 512
DMA_ISSUE_UNROLL = 8


def _tile(dim, target, quantum=V7X_LANES):
    if dim <= target:
        return dim
    t = (target // quantum) * quantum
    while t > quantum and dim % t:
        t -= quantum
    assert dim % t == 0, (dim, target)
    return t


def _params(semantics, vmem_bytes):
    limit = min(V7X_VMEM_BUDGET, max(32 * 1024 * 1024, int(vmem_bytes)))
    return pltpu.CompilerParams(dimension_semantics=semantics, vmem_limit_bytes=limit)


def _dot(a, b):
    return jnp.dot(a, b, preferred_element_type=F32)


def _layer_norm(z, g, b):
    mu = jnp.mean(z, axis=-1, keepdims=True)
    zc = z - mu
    var = jnp.mean(zc * zc, axis=-1, keepdims=True)
    return zc * lax.rsqrt(var + LN_EPS) * g + b


def _matmul_kernel(x_ref, w_ref, o_ref, *, scale):
    acc = _dot(x_ref[...], w_ref[...])
    if scale != 1.0:
        acc = acc * scale
    o_ref[...] = acc.astype(o_ref.dtype)


def _matmul(x, w, *, out_dtype, scale=1.0, name):
    n, k = x.shape
    m = w.shape[1]
    tm = _tile(n, 1024)
    tn = _tile(m, 1024)
    vmem = 2 * (tm * k * 2 + k * tn * 2 + tm * tn * 4) + tm * tn * 4
    return pl.pallas_call(
        functools.partial(_matmul_kernel, scale=scale),
        grid=(m // tn, n // tm),
        in_specs=[pl.BlockSpec((tm, k), lambda j, i: (i, 0)),
                  pl.BlockSpec((k, tn), lambda j, i: (0, j))],
        out_specs=pl.BlockSpec((tm, tn), lambda j, i: (i, j)),
        out_shape=jax.ShapeDtypeStruct((n, m), out_dtype),
        compiler_params=_params(("parallel", "parallel"), vmem),
        name=name,
    )(x, w)


def _matmul_nt_kernel(wt_ref, x_ref, o_ref, *, scale):
    acc = lax.dot_general(wt_ref[...], x_ref[...], (((1,), (1,)), ((), ())),
                          preferred_element_type=F32)
    if scale != 1.0:
        acc = acc * scale
    o_ref[...] = acc.astype(o_ref.dtype)


def _matmul_nt(wt, x, *, out_dtype, scale=1.0, name):
    m, k = wt.shape
    n = x.shape[0]
    tm = _tile(n, 1024)
    tn = _tile(m, 1024)
    vmem = 2 * (tm * k * 2 + k * tn * 2 + tm * tn * 4) + tm * tn * 4
    return pl.pallas_call(
        functools.partial(_matmul_nt_kernel, scale=scale),
        grid=(m // tn, n // tm),
        in_specs=[pl.BlockSpec((tn, k), lambda j, i: (j, 0)),
                  pl.BlockSpec((tm, k), lambda j, i: (i, 0))],
        out_specs=pl.BlockSpec((tn, tm), lambda j, i: (j, i)),
        out_shape=jax.ShapeDtypeStruct((m, n), out_dtype),
        compiler_params=_params(("parallel", "parallel"), vmem),
        name=name,
    )(wt, x)


def _conv_gate_kernel(x_ref, wb_ref, wc_ref, wh_ref, cw_ref, o_ref, ubuf_ref, *, tiles_per_seq):
    i = pl.program_id(1)
    tm = x_ref.shape[0]
    pad = V7X_SUBLANES

    @pl.when(i % tiles_per_seq == 0)
    def _():
        ubuf_ref[0:pad, :] = jnp.zeros((pad, ubuf_ref.shape[1]), F32)

    x = x_ref[...]
    u = _dot(x, wc_ref[...].astype(BF16)) * _dot(x, wh_ref[...].astype(BF16))
    ubuf_ref[pad:pad + tm, :] = u
    u1 = ubuf_ref[pad - 1:pad - 1 + tm, :]
    u2 = ubuf_ref[pad - 2:pad - 2 + tm, :]
    cw = cw_ref[...]
    conv = cw[0:1, :] * u2 + cw[1:2, :] * u1 + cw[2:3, :] * u
    o_ref[...] = (_dot(x, wb_ref[...].astype(BF16)) * conv).astype(o_ref.dtype)
    ubuf_ref[0:pad, :] = ubuf_ref[tm:tm + pad, :]


def _conv_gate(xb, w_in, conv_w, layer, *, seq):
    n, d = xb.shape
    tm = _tile(seq, 1024)
    tn = _tile(d, 512)
    nj = d // tn
    vmem = 2 * (tm * d * 2 + 3 * d * tn * 4 + tm * tn * 2) + (tm + 8) * tn * 4 + 6 * tm * tn * 4
    w_spec = lambda third: pl.BlockSpec((None, d, tn), lambda j, i: (layer, 0, third * nj + j))
    return pl.pallas_call(
        functools.partial(_conv_gate_kernel, tiles_per_seq=seq // tm),
        grid=(nj, n // tm),
        in_specs=[pl.BlockSpec((tm, d), lambda j, i: (i, 0)),
                  w_spec(0), w_spec(1), w_spec(2),
                  pl.BlockSpec((None, 3, tn), lambda j, i: (layer, 0, j))],
        out_specs=pl.BlockSpec((tm, tn), lambda j, i: (i, j)),
        out_shape=jax.ShapeDtypeStruct((n, d), BF16),
        scratch_shapes=[pltpu.VMEM((tm + V7X_SUBLANES, tn), F32)],
        compiler_params=_params(("parallel", "arbitrary"), vmem),
        name="conv_gate",
    )(xb, w_in, w_in, w_in, conv_w)


def _pack_bf16_pairs(y):
    half = y.shape[1] // 2
    lo = lax.bitcast_convert_type(y[:, :half].astype(BF16).astype(F32), jnp.uint32)
    hi = lax.bitcast_convert_type(y[:, half:].astype(BF16).astype(F32), jnp.uint32)
    return (lo >> 16) | (hi & jnp.uint32(0xFFFF0000))


def _unpack_bf16_pairs(p):
    lo = lax.bitcast_convert_type(p << 16, F32).astype(BF16)
    hi = lax.bitcast_convert_type(p & jnp.uint32(0xFFFF0000), F32).astype(BF16)
    return lo, hi


def _proj_ln_kernel(a_ref, w_ref, res_ref, g_ref, b_ref, of_ref, ob_ref, *, alpha, nk, row_chunks, packed):
    k = pl.program_id(1)
    rc = a_ref.shape[0] // row_chunks

    def finish(rows, acc):
        y = _layer_norm(alpha * res_ref[rows, :] + acc, g_ref[...], b_ref[...])
        of_ref[rows, :] = y
        ob_ref[rows, :] = _pack_bf16_pairs(y) if packed else y.astype(BF16)

    if nk == 1:
        for c in range(row_chunks):
            rows = pl.ds(c * rc, rc)
            finish(rows, _dot(a_ref[rows, :], w_ref[...]))
        return

    @pl.when(k == 0)
    def _():
        of_ref[...] = jnp.zeros_like(of_ref)

    of_ref[...] += _dot(a_ref[...], w_ref[...])

    @pl.when(k == nk - 1)
    def _():
        for c in range(row_chunks):
            rows = pl.ds(c * rc, rc)
            finish(rows, of_ref[rows, :])


def _proj_ln(a, w, res, g, b, *, alpha, name, packed=False):
    n, kdim = a.shape
    d = w.shape[1]
    split = kdim > 2048
    tm = _tile(n, 512)
    tk = _tile(kdim, 2816) if split else kdim
    nk = kdim // tk
    row_chunks = 2
    vmem = (2 * (tm * tk * 2 + tk * d * 2 + tm * d * 4 + tm * d * 6)
            + (tm * d * 4 if split else 0) + 3 * (tm // row_chunks) * d * 4)
    ob_shape = jax.ShapeDtypeStruct((n, d // 2), jnp.uint32) if packed else jax.ShapeDtypeStruct((n, d), BF16)
    return pl.pallas_call(
        functools.partial(_proj_ln_kernel, alpha=alpha, nk=nk, row_chunks=row_chunks, packed=packed),
        grid=(n // tm, nk),
        in_specs=[pl.BlockSpec((tm, tk), lambda i, k: (i, k)),
                  pl.BlockSpec((tk, d), lambda i, k: (k, 0)),
                  pl.BlockSpec((tm, d), lambda i, k: (i, 0)),
                  pl.BlockSpec((1, d), lambda i, k: (0, 0)),
                  pl.BlockSpec((1, d), lambda i, k: (0, 0))],
        out_specs=[pl.BlockSpec((tm, d), lambda i, k: (i, 0)),
                   pl.BlockSpec((tm, ob_shape.shape[1]), lambda i, k: (i, 0))],
        out_shape=[jax.ShapeDtypeStruct((n, d), F32), ob_shape],
        compiler_params=_params(("parallel", "arbitrary"), vmem),
        name=name,
    )(a, w, res, g, b)


def _swiglu_up_kernel(x_ref, wg_ref, wu_ref, o_ref, wg16_ref, wu16_ref):
    @pl.when(pl.program_id(1) == 0)
    def _():
        wg16_ref[...] = wg_ref[...].astype(BF16)
        wu16_ref[...] = wu_ref[...].astype(BF16)

    x = x_ref[...]
    o_ref[...] = (jax.nn.silu(_dot(x, wg16_ref[...])) * _dot(x, wu16_ref[...])).astype(o_ref.dtype)


def _swiglu_up(xb, wg, wu, layer):
    n, d = xb.shape
    f = wg.shape[2]
    tm = _tile(n, 1024)
    tf = _tile(f, 512)
    vmem = 2 * (tm * d * 2 + 2 * d * tf * 4 + tm * tf * 2) + 2 * d * tf * 2 + 4 * tm * tf * 4
    w_spec = pl.BlockSpec((None, d, tf), lambda j, i: (layer, 0, j))
    return pl.pallas_call(
        _swiglu_up_kernel,
        grid=(f // tf, n // tm),
        in_specs=[pl.BlockSpec((tm, d), lambda j, i: (i, 0)), w_spec, w_spec],
        out_specs=pl.BlockSpec((tm, tf), lambda j, i: (i, j)),
        out_shape=jax.ShapeDtypeStruct((n, f), BF16),
        scratch_shapes=[pltpu.VMEM((d, tf), BF16), pltpu.VMEM((d, tf), BF16)],
        compiler_params=_params(("arbitrary", "arbitrary"), vmem),
        name="swiglu_up",
    )(xb, wg, wu)


def _attn_kernel(sink_ref, qt_ref, kp_ref, kc_ref, vtp_ref, vtc_ref, o_ref, *, n_heads):
    blk = o_ref.shape[0]
    key = lax.broadcasted_iota(jnp.int32, (blk, blk), 0)
    qry = lax.broadcasted_iota(jnp.int32, (blk, blk), 1)
    from_cur = key <= qry
    prev_bias = jnp.where(pl.program_id(1) > 0, 0.0, NEG_INF)
    heads_per_store = V7X_LANES // HEAD_DIM
    done = []
    for h in range(n_heads):
        qs = slice(h * HEAD_DIM, (h + 1) * HEAD_DIM)
        ks = slice((h // GQA_GROUP) * HEAD_DIM, (h // GQA_GROUP + 1) * HEAD_DIM)
        qt = qt_ref[qs, :]
        s = jnp.where(from_cur, _dot(kc_ref[:, ks], qt), _dot(kp_ref[:, ks], qt) + prev_bias)
        sink = sink_ref[h]
        m = jnp.maximum(jnp.max(s, axis=0, keepdims=True), sink)
        p = jnp.exp(s - m)
        denom = jnp.sum(p, axis=0, keepdims=True) + jnp.exp(sink - m)
        p_c = jnp.where(from_cur, p, 0.0).astype(BF16)
        p_p = jnp.where(from_cur, 0.0, p).astype(BF16)
        ot = _dot(vtc_ref[ks, :], p_c) + _dot(vtp_ref[ks, :], p_p)
        done.append(ot / denom)
        if len(done) == heads_per_store:
            first = h + 1 - heads_per_store
            o_ref[:, first * HEAD_DIM:(h + 1) * HEAD_DIM] = (
                jnp.concatenate(done, axis=0).T.astype(o_ref.dtype))
            done = []


def _attention(qt, k, vt, sinks, *, bsz, seq):
    d, n = qt.shape
    kvd = k.shape[1]
    nb = seq // ATTN_BLOCK
    cur_row = lambda b, i: (b * nb + i, 0)
    prev_row = lambda b, i: (b * nb + jnp.maximum(i - 1, 0), 0)
    cur_col = lambda b, i: (0, b * nb + i)
    prev_col = lambda b, i: (0, b * nb + jnp.maximum(i - 1, 0))
    vmem = 2 * (2 * ATTN_BLOCK * d * 2 + 4 * ATTN_BLOCK * kvd * 2) + 64 * ATTN_BLOCK * ATTN_BLOCK * 4
    return pl.pallas_call(
        functools.partial(_attn_kernel, n_heads=d // HEAD_DIM),
        grid=(bsz, nb),
        in_specs=[pl.BlockSpec(memory_space=pltpu.SMEM),
                  pl.BlockSpec((d, ATTN_BLOCK), cur_col),
                  pl.BlockSpec((ATTN_BLOCK, kvd), prev_row),
                  pl.BlockSpec((ATTN_BLOCK, kvd), cur_row),
                  pl.BlockSpec((kvd, ATTN_BLOCK), prev_col),
                  pl.BlockSpec((kvd, ATTN_BLOCK), cur_col)],
        out_specs=pl.BlockSpec((ATTN_BLOCK, d), cur_row),
        out_shape=jax.ShapeDtypeStruct((n, d), BF16),
        compiler_params=_params(("parallel", "parallel"), vmem),
        name="swa_attention",
    )(sinks, qt, k, k, vt, vt)


def _router_kernel(x_ref, wr_ref, ri_ref, rg_ref, cnt_ref, carry_ref, *, n_experts):
    i = pl.program_id(0)
    tm = x_ref.shape[0]

    @pl.when(i == 0)
    def _():
        carry_ref[...] = jnp.zeros_like(carry_ref)

    x = x_ref[...]
    w = wr_ref[...]
    xh = x.astype(BF16)
    wh = w.astype(BF16)
    xl = (x - xh.astype(F32)).astype(BF16)
    wl = (w - wh.astype(F32)).astype(BF16)
    logits = _dot(xh, wh) + (_dot(xl, wh) + _dot(xh, wl))
    lane = lax.broadcasted_iota(jnp.int32, logits.shape, 1)
    logits = jnp.where(lane < n_experts, logits, -jnp.inf)
    m1 = jnp.max(logits, axis=-1, keepdims=True)
    i1 = jnp.min(jnp.where(logits == m1, lane, V7X_LANES), axis=-1, keepdims=True)
    rest = jnp.where(lane == i1, -jnp.inf, logits)
    m2 = jnp.max(rest, axis=-1, keepdims=True)
    i2 = jnp.min(jnp.where(rest == m2, lane, V7X_LANES), axis=-1, keepdims=True)
    e2 = jnp.exp(m2 - m1)
    g1 = 1.0 / (1.0 + e2)
    g2 = e2 / (1.0 + e2)

    sel1 = lane == i1
    sel2 = lane == i2
    member = jnp.logical_or(sel1, sel2).astype(BF16)
    r = lax.broadcasted_iota(jnp.int32, (tm, tm), 0)
    c = lax.broadcasted_iota(jnp.int32, (tm, tm), 1)
    before = (c < r).astype(BF16)
    prior = carry_ref[...] + _dot(before, member)
    r1 = jnp.sum(jnp.where(sel1, prior, 0.0), axis=-1, keepdims=True).astype(jnp.int32)
    r2 = jnp.sum(jnp.where(sel2, prior, 0.0), axis=-1, keepdims=True).astype(jnp.int32)
    carry_ref[...] += jnp.sum(member.astype(F32), axis=0, keepdims=True)

    ri_ref[...] = jnp.where(lane == 0, i1, jnp.where(lane == 1, i2,
                            jnp.where(lane == 2, r1, jnp.where(lane == 3, r2, 0))))
    rg_ref[...] = jnp.where(lane == 0, g1, jnp.where(lane == 1, g2, 0.0))
    cnt_ref[...] = carry_ref[...].astype(jnp.int32)


def _router(x, w_router_padded, *, n_experts):
    n, d = x.shape
    tm = _tile(n, 512)
    vmem = 2 * (tm * d * 4 + d * V7X_LANES * 4 + 2 * tm * V7X_LANES * 4) + 8 * tm * tm * 4
    return pl.pallas_call(
        functools.partial(_router_kernel, n_experts=n_experts),
        grid=(n // tm,),
        in_specs=[pl.BlockSpec((tm, d), lambda i: (i, 0)),
                  pl.BlockSpec((d, V7X_LANES), lambda i: (0, 0))],
        out_specs=[pl.BlockSpec((tm, V7X_LANES), lambda i: (i, 0)),
                   pl.BlockSpec((tm, V7X_LANES), lambda i: (i, 0)),
                   pl.BlockSpec((1, V7X_LANES), lambda i: (0, 0))],
        out_shape=[jax.ShapeDtypeStruct((n, V7X_LANES), jnp.int32),
                   jax.ShapeDtypeStruct((n, V7X_LANES), F32),
                   jax.ShapeDtypeStruct((1, V7X_LANES), jnp.int32)],
        scratch_shapes=[pltpu.VMEM((1, V7X_LANES), F32)],
        compiler_params=_params(("arbitrary",), vmem),
        name="moe_router",
    )(x, w_router_padded)


def _dispatch_kernel(slot_ref, x_ref, init_hbm, xs_hbm, sem):
    del init_hbm
    chunk = x_ref.shape[0]

    def copies(r):
        return [pltpu.make_async_copy(x_ref.at[pl.ds(r, 1), :],
                                      xs_hbm.at[pl.ds(slot_ref[0, 0, TOP_K * r + k], 1), :], sem)
                for k in range(TOP_K)]

    def start(r, carry):
        for cp in copies(r):
            cp.start()
        return carry

    lax.fori_loop(0, chunk, start, 0, unroll=DMA_ISSUE_UNROLL)
    for _ in range(TOP_K):
        pltpu.make_async_copy(x_ref, xs_hbm.at[pl.ds(0, chunk), :], sem).wait()


def _dispatch(x, slots, *, n_slots):
    n, d = x.shape
    chunk = _tile(n, 512)
    slots3 = slots.reshape(n // chunk, 1, TOP_K * chunk)
    init = jnp.zeros((n_slots, d), x.dtype)
    return pl.pallas_call(
        _dispatch_kernel,
        grid=(n // chunk,),
        in_specs=[pl.BlockSpec((1, 1, TOP_K * chunk), lambda i: (i, 0, 0), memory_space=pltpu.SMEM),
                  pl.BlockSpec((chunk, d), lambda i: (i, 0)),
                  pl.BlockSpec(memory_space=pl.ANY)],
        out_specs=pl.BlockSpec(memory_space=pl.ANY),
        out_shape=jax.ShapeDtypeStruct((n_slots, d), x.dtype),
        scratch_shapes=[pltpu.SemaphoreType.DMA(())],
        input_output_aliases={2: 0},
        compiler_params=_params(("arbitrary",), 2 * chunk * d * 4),
        name="moe_dispatch",
    )(slots3, x, init)


def _moe_up_kernel(te_ref, rows_ref, xs_ref, wg_ref, wu_ref, o_ref):
    del te_ref
    t = pl.program_id(1)

    @pl.when(rows_ref[t] > 0)
    def _():
        x = jnp.concatenate(_unpack_bf16_pairs(xs_ref[...]), axis=1)
        gate = _dot(x, wg_ref[...].astype(BF16))
        up = _dot(x, wu_ref[...].astype(BF16))
        o_ref[...] = (jax.nn.silu(gate) * up).astype(o_ref.dtype)

    @pl.when(rows_ref[t] == 0)
    def _():
        o_ref[...] = jnp.zeros_like(o_ref)


def _moe_up(xs, w_gate, w_up, layer, tile_expert, tile_rows):
    s = xs.shape[0]
    d = 2 * xs.shape[1]
    f = w_gate.shape[3]
    tm = MOE_TILE
    tf = _tile(f, 1024)
    vmem = 2 * (tm * d * 2 + 2 * d * tf * 4 + tm * tf * 2) + 4 * tm * tf * 4
    w_spec = pl.BlockSpec((None, None, d, tf), lambda j, t, te, rows: (layer, te[t], 0, j))
    grid_spec = pltpu.PrefetchScalarGridSpec(
        num_scalar_prefetch=2,
        grid=(f // tf, s // tm),
        in_specs=[pl.BlockSpec((tm, d // 2), lambda j, t, te, rows: (t, 0)), w_spec, w_spec],
        out_specs=pl.BlockSpec((tm, tf), lambda j, t, te, rows: (t, j)),
    )
    return pl.pallas_call(
        _moe_up_kernel,
        grid_spec=grid_spec,
        out_shape=jax.ShapeDtypeStruct((s, f), BF16),
        compiler_params=_params(("arbitrary", "arbitrary"), vmem),
        name="moe_up",
    )(tile_expert, tile_rows, xs, w_gate, w_up)


def _moe_down_kernel(te_ref, rows_ref, h_ref, wd_ref, o_ref):
    del te_ref
    t = pl.program_id(1)

    @pl.when(rows_ref[t] > 0)
    def _():
        o_ref[...] = _dot(h_ref[...], wd_ref[...].astype(BF16))

    @pl.when(rows_ref[t] == 0)
    def _():
        o_ref[...] = jnp.zeros_like(o_ref)


def _moe_down(h, w_down, layer, tile_expert, tile_rows):
    s, f = h.shape
    d = w_down.shape[3]
    tm = MOE_TILE
    tn = _tile(d, 512)
    vmem = 2 * (tm * f * 2 + tm * tn * 4 + f * tn * 4) + 2 * tm * tn * 4
    grid_spec = pltpu.PrefetchScalarGridSpec(
        num_scalar_prefetch=2,
        grid=(d // tn, s // tm),
        in_specs=[pl.BlockSpec((tm, f), lambda j, t, te, rows: (t, 0)),
                  pl.BlockSpec((None, None, f, tn), lambda j, t, te, rows: (layer, te[t], 0, j))],
        out_specs=pl.BlockSpec((tm, tn), lambda j, t, te, rows: (t, j)),
    )
    return pl.pallas_call(
        _moe_down_kernel,
        grid_spec=grid_spec,
        out_shape=jax.ShapeDtypeStruct((s, d), F32),
        compiler_params=_params(("arbitrary", "arbitrary"), vmem),
        name="moe_down",
    )(tile_expert, tile_rows, h, w_down)


def _combine_ln_kernel(slot_ref, next_slot_ref, y_hbm, rg_ref, res_ref, g_ref, b_ref, of_ref, ob_ref,
                       buf_ref, sem, *, alpha):
    i = pl.program_id(0)
    tm = res_ref.shape[0]

    def gather(slots, parity):
        def start(r, carry):
            for k in range(TOP_K):
                pltpu.make_async_copy(y_hbm.at[pl.ds(slots[0, 0, TOP_K * r + k], 1), :],
                                      buf_ref.at[parity * TOP_K + k, pl.ds(r, 1), :],
                                      sem.at[parity]).start()
            return carry
        lax.fori_loop(0, tm, start, 0, unroll=DMA_ISSUE_UNROLL)

    @pl.when(i == 0)
    def _():
        gather(slot_ref, 0)

    @pl.when(i + 1 < pl.num_programs(0))
    def _():
        gather(next_slot_ref, (i + 1) % 2)

    parity = i % 2
    for k in range(TOP_K):
        pltpu.make_async_copy(y_hbm.at[pl.ds(0, tm), :], buf_ref.at[parity * TOP_K + k],
                              sem.at[parity]).wait()
    gates = rg_ref[...]
    ff = gates[:, 0:1] * buf_ref[parity * TOP_K] + gates[:, 1:2] * buf_ref[parity * TOP_K + 1]
    y = _layer_norm(alpha * res_ref[...] + ff, g_ref[...], b_ref[...])
    of_ref[...] = y
    ob_ref[...] = y.astype(BF16)


def _combine_ln(y, slots, gates, res, g, b, *, alpha):
    n, d = res.shape
    tm = _tile(n, 256)
    steps = n // tm
    slots3 = slots.reshape(steps, 1, TOP_K * tm)
    vmem = (2 * (tm * V7X_LANES * 4 + tm * d * 4 + tm * d * 6) + 2 * TOP_K * tm * d * 4
            + 3 * tm * d * 4)
    slot_block = (1, 1, TOP_K * tm)
    return pl.pallas_call(
        functools.partial(_combine_ln_kernel, alpha=alpha),
        grid=(steps,),
        in_specs=[pl.BlockSpec(slot_block, lambda i: (i, 0, 0), memory_space=pltpu.SMEM),
                  pl.BlockSpec(slot_block, lambda i: (jnp.minimum(i + 1, steps - 1), 0, 0),
                               memory_space=pltpu.SMEM),
                  pl.BlockSpec(memory_space=pl.ANY),
                  pl.BlockSpec((tm, V7X_LANES), lambda i: (i, 0)),
                  pl.BlockSpec((tm, d), lambda i: (i, 0)),
                  pl.BlockSpec((1, d), lambda i: (0, 0)),
                  pl.BlockSpec((1, d), lambda i: (0, 0))],
        out_specs=[pl.BlockSpec((tm, d), lambda i: (i, 0)),
                   pl.BlockSpec((tm, d), lambda i: (i, 0))],
        out_shape=[jax.ShapeDtypeStruct((n, d), F32), jax.ShapeDtypeStruct((n, d), BF16)],
        scratch_shapes=[pltpu.VMEM((2 * TOP_K, tm, d), F32), pltpu.SemaphoreType.DMA((2,))],
        compiler_params=_params(("arbitrary",), vmem),
        name="moe_combine_ln",
    )(slots3, slots3, y, gates, res, g, b)


def _moe_layer(x, x_packed, w_router, w_gate, w_up, w_down, layer, g, b, *, alpha):
    n, d = x.shape
    n_experts = w_router.shape[1]
    tm = MOE_TILE
    n_tiles = (TOP_K * n) // tm + n_experts
    wr = jnp.pad(w_router, ((0, 0), (0, V7X_LANES - n_experts)))
    ri, rg, cnt = _router(x, wr, n_experts=n_experts)

    counts = cnt[0, :n_experts]
    tiles_per = (counts + tm - 1) // tm
    tile_end = jnp.cumsum(tiles_per)
    n_used = tile_end[-1]
    group_start = (tile_end - tiles_per) * tm
    tile_id = jnp.arange(n_tiles, dtype=jnp.int32)
    last_used = jnp.minimum(tile_id, jnp.maximum(n_used - 1, 0))
    tile_expert = jnp.sum((last_used[:, None] >= tile_end[None, :]).astype(jnp.int32), axis=1)
    tile_expert = jnp.minimum(tile_expert, n_experts - 1)
    slots = (group_start[ri[:, :TOP_K]] + ri[:, TOP_K:2 * TOP_K]).astype(jnp.int32).reshape(-1)
    group_end = group_start + counts
    tile_rows = jnp.clip(group_end[tile_expert] - tile_id * tm, 0, tm)
    tile_rows = jnp.where(tile_id < n_used, tile_rows, 0).astype(jnp.int32)

    xs = _dispatch(x_packed, slots, n_slots=n_tiles * tm)
    h = _moe_up(xs, w_gate, w_up, layer, tile_expert, tile_rows)
    y = _moe_down(h, w_down, layer, tile_expert, tile_rows)
    return _combine_ln(y, slots, rg, x, g, b, alpha=alpha)


def kernel(x, conv_w_in, conv_w, conv_w_out, w_kv, attn_w_q, attn_sinks, attn_w_o, ln_g, ln_b,
           ffn_w_gate, ffn_w_up, ffn_w_down, moe_w_router, moe_w_gate, moe_w_up, moe_w_down):
    bsz, seq, d = x.shape
    n = bsz * seq
    depth = ln_g.shape[0]
    n_a = conv_w_in.shape[0]
    alpha = float((2 * depth) ** 0.25)
    assert seq % ATTN_BLOCK == 0 and d % (HEAD_DIM * GQA_GROUP) == 0
    assert w_kv.shape[1] == 2 * d // GQA_GROUP

    xf = x.reshape(n, d)
    xb = xf.astype(BF16)
    k_shared = vt_shared = None
    for l in range(depth):
        g0, b0 = ln_g[l, 0].reshape(1, d), ln_b[l, 0].reshape(1, d)
        g1, b1 = ln_g[l, 1].reshape(1, d), ln_b[l, 1].reshape(1, d)
        moe = l % 2 == 1
        if l < n_a:
            gated = _conv_gate(xb, conv_w_in, conv_w, l, seq=seq)
            xf, xb = _proj_ln(gated, conv_w_out[l].astype(BF16), xf, g0, b0, alpha=alpha,
                              name="conv_out_ln", packed=moe)
        else:
            j = l - n_a
            qt = _matmul_nt(attn_w_q[j].T.astype(BF16), xb, out_dtype=BF16,
                            scale=1.0 / math.sqrt(HEAD_DIM), name="attn_q")
            o = _attention(qt, k_shared, vt_shared, attn_sinks[j], bsz=bsz, seq=seq)
            xf, xb = _proj_ln(o, attn_w_o[j].astype(BF16), xf, g0, b0, alpha=alpha,
                              name="attn_out_ln", packed=moe)
        i = l // 2
        if moe:
            xf, xb = _moe_layer(xf, xb, moe_w_router[i], moe_w_gate, moe_w_up, moe_w_down, i,
                                g1, b1, alpha=alpha)
        else:
            h = _swiglu_up(xb, ffn_w_gate, ffn_w_up, i)
            xf, xb = _proj_ln(h, ffn_w_down[i].astype(BF16), xf, g1, b1, alpha=alpha, name="ffn_down_ln")
        if l == n_a - 1:
            kvd = w_kv.shape[1] // 2
            k_shared = _matmul(xb, w_kv[:, :kvd].astype(BF16), out_dtype=BF16, name="k_proj")
            vt_shared = _matmul_nt(w_kv[:, kvd:].T.astype(BF16), xb, out_dtype=BF16, name="v_proj")
    return xf.reshape(bsz, seq, d)
```

```python
import functools
import math

import jax
import jax.numpy as jnp
from jax import lax
from jax.experimental import pallas as pl
from jax.experimental.pallas import tpu as pltpu

F32 = jnp.float32
BF16 = jnp.bfloat16

HEAD_DIM = 64
GQA_GROUP = 8
ATTN_BLOCK = 128
TOP_K = 2
LN_EPS = 1e-5
NEG_INF = -1e30

V7X_LANES = 128
V7X_SUBLANES = 8
V7X_VMEM_BYTES = 64 * 1024 * 1024
V7X_VMEM_BUDGET = V7X_VMEM_BYTES - 8 * 1024 * 1024

MOE_TILE = 512
MOE_H_BUFFERS = 3
DMA_ISSUE_UNROLL = 8


def _tile(dim, target, quantum=V7X_LANES):
    if dim <= target:
        return dim
    t = (target // quantum) * quantum
    while t > quantum and dim % t:
        t -= quantum
    assert dim % t == 0, (dim, target)
    return t


def _params(semantics, vmem_bytes):
    limit = min(V7X_VMEM_BUDGET, max(32 * 1024 * 1024, int(vmem_bytes)))
    return pltpu.CompilerParams(dimension_semantics=semantics, vmem_limit_bytes=limit)


def _dot(a, b):
    return jnp.dot(a, b, preferred_element_type=F32)


def _layer_norm(z, g, b):
    mu = jnp.mean(z, axis=-1, keepdims=True)
    zc = z - mu
    var = jnp.mean(zc * zc, axis=-1, keepdims=True)
    return zc * lax.rsqrt(var + LN_EPS) * g + b


def _matmul_kernel(x_ref, w_ref, o_ref, *, scale):
    acc = _dot(x_ref[...], w_ref[...])
    if scale != 1.0:
        acc = acc * scale
    o_ref[...] = acc.astype(o_ref.dtype)


def _matmul(x, w, *, out_dtype, scale=1.0, name):
    n, k = x.shape
    m = w.shape[1]
    tm = _tile(n, 1024)
    tn = _tile(m, 1024)
    vmem = 2 * (tm * k * 2 + k * tn * 2 + tm * tn * 4) + tm * tn * 4
    return pl.pallas_call(
        functools.partial(_matmul_kernel, scale=scale),
        grid=(m // tn, n // tm),
        in_specs=[pl.BlockSpec((tm, k), lambda j, i: (i, 0)),
                  pl.BlockSpec((k, tn), lambda j, i: (0, j))],
        out_specs=pl.BlockSpec((tm, tn), lambda j, i: (i, j)),
        out_shape=jax.ShapeDtypeStruct((n, m), out_dtype),
        compiler_params=_params(("parallel", "parallel"), vmem),
        name=name,
    )(x, w)


def _matmul_nt_kernel(wt_ref, x_ref, o_ref, *, scale):
    acc = lax.dot_general(wt_ref[...], x_ref[...], (((1,), (1,)), ((), ())),
                          preferred_element_type=F32)
    if scale != 1.0:
        acc = acc * scale
    o_ref[...] = acc.astype(o_ref.dtype)


def _matmul_nt(wt, x, *, out_dtype, scale=1.0, name):
    m, k = wt.shape
    n = x.shape[0]
    tm = _tile(n, 1024)
    tn = _tile(m, 1024)
    vmem = 2 * (tm * k * 2 + k * tn * 2 + tm * tn * 4) + tm * tn * 4
    return pl.pallas_call(
        functools.partial(_matmul_nt_kernel, scale=scale),
        grid=(m // tn, n // tm),
        in_specs=[pl.BlockSpec((tn, k), lambda j, i: (j, 0)),
                  pl.BlockSpec((tm, k), lambda j, i: (i, 0))],
        out_specs=pl.BlockSpec((tn, tm), lambda j, i: (j, i)),
        out_shape=jax.ShapeDtypeStruct((m, n), out_dtype),
        compiler_params=_params(("parallel", "parallel"), vmem),
        name=name,
    )(wt, x)


def _conv_gate_kernel(x_ref, wb_ref, wc_ref, wh_ref, cw_ref, o_ref, ubuf_ref, *, tiles_per_seq):
    i = pl.program_id(1)
    tm = x_ref.shape[0]
    pad = V7X_SUBLANES

    @pl.when(i % tiles_per_seq == 0)
    def _():
        ubuf_ref[0:pad, :] = jnp.zeros((pad, ubuf_ref.shape[1]), F32)

    x = x_ref[...]
    u = _dot(x, wc_ref[...].astype(BF16)) * _dot(x, wh_ref[...].astype(BF16))
    ubuf_ref[pad:pad + tm, :] = u
    u1 = ubuf_ref[pad - 1:pad - 1 + tm, :]
    u2 = ubuf_ref[pad - 2:pad - 2 + tm, :]
    cw = cw_ref[...]
    conv = cw[0:1, :] * u2 + cw[1:2, :] * u1 + cw[2:3, :] * u
    o_ref[...] = (_dot(x, wb_ref[...].astype(BF16)) * conv).astype(o_ref.dtype)
    ubuf_ref[0:pad, :] = ubuf_ref[tm:tm + pad, :]


def _conv_gate(xb, w_in, conv_w, layer, *, seq):
    n, d = xb.shape
    tm = _tile(seq, 1024)
    tn = _tile(d, 512)
    nj = d // tn
    vmem = 2 * (tm * d * 2 + 3 * d * tn * 4 + tm * tn * 2) + (tm + 8) * tn * 4 + 6 * tm * tn * 4
    w_spec = lambda third: pl.BlockSpec((None, d, tn), lambda j, i: (layer, 0, third * nj + j))
    return pl.pallas_call(
        functools.partial(_conv_gate_kernel, tiles_per_seq=seq // tm),
        grid=(nj, n // tm),
        in_specs=[pl.BlockSpec((tm, d), lambda j, i: (i, 0)),
                  w_spec(0), w_spec(1), w_spec(2),
                  pl.BlockSpec((None, 3, tn), lambda j, i: (layer, 0, j))],
        out_specs=pl.BlockSpec((tm, tn), lambda j, i: (i, j)),
        out_shape=jax.ShapeDtypeStruct((n, d), BF16),
        scratch_shapes=[pltpu.VMEM((tm + V7X_SUBLANES, tn), F32)],
        compiler_params=_params(("parallel", "arbitrary"), vmem),
        name="conv_gate",
    )(xb, w_in, w_in, w_in, conv_w)


def _pack_bf16_pairs(y):
    half = y.shape[1] // 2
    lo = lax.bitcast_convert_type(y[:, :half].astype(BF16).astype(F32), jnp.uint32)
    hi = lax.bitcast_convert_type(y[:, half:].astype(BF16).astype(F32), jnp.uint32)
    return (lo >> 16) | (hi & jnp.uint32(0xFFFF0000))


def _unpack_bf16_pairs(p):
    lo = lax.bitcast_convert_type(p << 16, F32).astype(BF16)
    hi = lax.bitcast_convert_type(p & jnp.uint32(0xFFFF0000), F32).astype(BF16)
    return lo, hi


def _proj_ln_kernel(a_ref, w_ref, res_ref, g_ref, b_ref, of_ref, ob_ref, *, alpha, nk, row_chunks, packed):
    k = pl.program_id(1)
    rc = a_ref.shape[0] // row_chunks

    def finish(rows, acc):
        y = _layer_norm(alpha * res_ref[rows, :] + acc, g_ref[...], b_ref[...])
        of_ref[rows, :] = y
        ob_ref[rows, :] = _pack_bf16_pairs(y) if packed else y.astype(BF16)

    if nk == 1:
        for c in range(row_chunks):
            rows = pl.ds(c * rc, rc)
            finish(rows, _dot(a_ref[rows, :], w_ref[...]))
        return

    @pl.when(k == 0)
    def _():
        of_ref[...] = jnp.zeros_like(of_ref)

    of_ref[...] += _dot(a_ref[...], w_ref[...])

    @pl.when(k == nk - 1)
    def _():
        for c in range(row_chunks):
            rows = pl.ds(c * rc, rc)
            finish(rows, of_ref[rows, :])


def _proj_ln(a, w, res, g, b, *, alpha, name, packed=False):
    n, kdim = a.shape
    d = w.shape[1]
    split = kdim > 2048
    tm = _tile(n, 512)
    tk = _tile(kdim, 2816) if split else kdim
    nk = kdim // tk
    row_chunks = 2
    vmem = (2 * (tm * tk * 2 + tk * d * 2 + tm * d * 4 + tm * d * 6)
            + (tm * d * 4 if split else 0) + 3 * (tm // row_chunks) * d * 4)
    ob_shape = jax.ShapeDtypeStruct((n, d // 2), jnp.uint32) if packed else jax.ShapeDtypeStruct((n, d), BF16)
    return pl.pallas_call(
        functools.partial(_proj_ln_kernel, alpha=alpha, nk=nk, row_chunks=row_chunks, packed=packed),
        grid=(n // tm, nk),
        in_specs=[pl.BlockSpec((tm, tk), lambda i, k: (i, k)),
                  pl.BlockSpec((tk, d), lambda i, k: (k, 0)),
                  pl.BlockSpec((tm, d), lambda i, k: (i, 0)),
                  pl.BlockSpec((1, d), lambda i, k: (0, 0)),
                  pl.BlockSpec((1, d), lambda i, k: (0, 0))],
        out_specs=[pl.BlockSpec((tm, d), lambda i, k: (i, 0)),
                   pl.BlockSpec((tm, ob_shape.shape[1]), lambda i, k: (i, 0))],
        out_shape=[jax.ShapeDtypeStruct((n, d), F32), ob_shape],
        compiler_params=_params(("parallel", "arbitrary"), vmem),
        name=name,
    )(a, w, res, g, b)


def _swiglu_up_kernel(x_ref, wg_ref, wu_ref, o_ref, wg16_ref, wu16_ref):
    @pl.when(pl.program_id(1) == 0)
    def _():
        wg16_ref[...] = wg_ref[...].astype(BF16)
        wu16_ref[...] = wu_ref[...].astype(BF16)

    x = x_ref[...]
    o_ref[...] = (jax.nn.silu(_dot(x, wg16_ref[...])) * _dot(x, wu16_ref[...])).astype(o_ref.dtype)


def _swiglu_up(xb, wg, wu, layer):
    n, d = xb.shape
    f = wg.shape[2]
    tm = _tile(n, 1024)
    tf = _tile(f, 512)
    vmem = 2 * (tm * d * 2 + 2 * d * tf * 4 + tm * tf * 2) + 2 * d * tf * 2 + 4 * tm * tf * 4
    w_spec = pl.BlockSpec((None, d, tf), lambda j, i: (layer, 0, j))
    return pl.pallas_call(
        _swiglu_up_kernel,
        grid=(f // tf, n // tm),
        in_specs=[pl.BlockSpec((tm, d), lambda j, i: (i, 0)), w_spec, w_spec],
        out_specs=pl.BlockSpec((tm, tf), lambda j, i: (i, j)),
        out_shape=jax.ShapeDtypeStruct((n, f), BF16),
        scratch_shapes=[pltpu.VMEM((d, tf), BF16), pltpu.VMEM((d, tf), BF16)],
        compiler_params=_params(("arbitrary", "arbitrary"), vmem),
        name="swiglu_up",
    )(xb, wg, wu)


def _attn_kernel(sink_ref, qt_ref, kp_ref, kc_ref, vtp_ref, vtc_ref, o_ref, *, n_heads):
    blk = o_ref.shape[0]
    key = lax.broadcasted_iota(jnp.int32, (blk, blk), 0)
    qry = lax.broadcasted_iota(jnp.int32, (blk, blk), 1)
    from_cur = key <= qry
    prev_bias = jnp.where(pl.program_id(1) > 0, 0.0, NEG_INF)
    heads_per_store = V7X_LANES // HEAD_DIM
    done = []
    for h in range(n_heads):
        qs = slice(h * HEAD_DIM, (h + 1) * HEAD_DIM)
        ks = slice((h // GQA_GROUP) * HEAD_DIM, (h // GQA_GROUP + 1) * HEAD_DIM)
        qt = qt_ref[qs, :]
        s = jnp.where(from_cur, _dot(kc_ref[:, ks], qt), _dot(kp_ref[:, ks], qt) + prev_bias)
        sink = sink_ref[h]
        m = jnp.maximum(jnp.max(s, axis=0, keepdims=True), sink)
        p = jnp.exp(s - m)
        denom = jnp.sum(p, axis=0, keepdims=True) + jnp.exp(sink - m)
        p_c = jnp.where(from_cur, p, 0.0).astype(BF16)
        p_p = jnp.where(from_cur, 0.0, p).astype(BF16)
        ot = _dot(vtc_ref[ks, :], p_c) + _dot(vtp_ref[ks, :], p_p)
        done.append(ot / denom)
        if len(done) == heads_per_store:
            first = h + 1 - heads_per_store
            o_ref[:, first * HEAD_DIM:(h + 1) * HEAD_DIM] = (
                jnp.concatenate(done, axis=0).T.astype(o_ref.dtype))
            done = []


def _attention(qt, k, vt, sinks, *, bsz, seq):
    d, n = qt.shape
    kvd = k.shape[1]
    nb = seq // ATTN_BLOCK
    cur_row = lambda b, i: (b * nb + i, 0)
    prev_row = lambda b, i: (b * nb + jnp.maximum(i - 1, 0), 0)
    cur_col = lambda b, i: (0, b * nb + i)
    prev_col = lambda b, i: (0, b * nb + jnp.maximum(i - 1, 0))
    vmem = 2 * (2 * ATTN_BLOCK * d * 2 + 4 * ATTN_BLOCK * kvd * 2) + 64 * ATTN_BLOCK * ATTN_BLOCK * 4
    return pl.pallas_call(
        functools.partial(_attn_kernel, n_heads=d // HEAD_DIM),
        grid=(bsz, nb),
        in_specs=[pl.BlockSpec(memory_space=pltpu.SMEM),
                  pl.BlockSpec((d, ATTN_BLOCK), cur_col),
                  pl.BlockSpec((ATTN_BLOCK, kvd), prev_row),
                  pl.BlockSpec((ATTN_BLOCK, kvd), cur_row),
                  pl.BlockSpec((kvd, ATTN_BLOCK), prev_col),
                  pl.BlockSpec((kvd, ATTN_BLOCK), cur_col)],
        out_specs=pl.BlockSpec((ATTN_BLOCK, d), cur_row),
        out_shape=jax.ShapeDtypeStruct((n, d), BF16),
        compiler_params=_params(("parallel", "parallel"), vmem),
        name="swa_attention",
    )(sinks, qt, k, k, vt, vt)


def _router_kernel(x_ref, wr_ref, ri_ref, rg_ref, cnt_ref, carry_ref, *, n_experts):
    i = pl.program_id(0)
    tm = x_ref.shape[0]

    @pl.when(i == 0)
    def _():
        carry_ref[...] = jnp.zeros_like(carry_ref)

    x = x_ref[...]
    w = wr_ref[...]
    xh = x.astype(BF16)
    wh = w.astype(BF16)
    xl = (x - xh.astype(F32)).astype(BF16)
    wl = (w - wh.astype(F32)).astype(BF16)
    logits = _dot(xh, wh) + (_dot(xl, wh) + _dot(xh, wl))
    lane = lax.broadcasted_iota(jnp.int32, logits.shape, 1)
    logits = jnp.where(lane < n_experts, logits, -jnp.inf)
    m1 = jnp.max(logits, axis=-1, keepdims=True)
    i1 = jnp.min(jnp.where(logits == m1, lane, V7X_LANES), axis=-1, keepdims=True)
    rest = jnp.where(lane == i1, -jnp.inf, logits)
    m2 = jnp.max(rest, axis=-1, keepdims=True)
    i2 = jnp.min(jnp.where(rest == m2, lane, V7X_LANES), axis=-1, keepdims=True)
    e2 = jnp.exp(m2 - m1)
    g1 = 1.0 / (1.0 + e2)
    g2 = e2 / (1.0 + e2)

    sel1 = lane == i1
    sel2 = lane == i2
    member = jnp.logical_or(sel1, sel2).astype(BF16)
    r = lax.broadcasted_iota(jnp.int32, (tm, tm), 0)
    c = lax.broadcasted_iota(jnp.int32, (tm, tm), 1)
    before = (c < r).astype(BF16)
    prior = carry_ref[...] + _dot(before, member)
    r1 = jnp.sum(jnp.where(sel1, prior, 0.0), axis=-1, keepdims=True).astype(jnp.int32)
    r2 = jnp.sum(jnp.where(sel2, prior, 0.0), axis=-1, keepdims=True).astype(jnp.int32)
    carry_ref[...] += jnp.sum(member.astype(F32), axis=0, keepdims=True)

    ri_ref[...] = jnp.where(lane == 0, i1, jnp.where(lane == 1, i2,
                            jnp.where(lane == 2, r1, jnp.where(lane == 3, r2, 0))))
    rg_ref[...] = jnp.where(lane == 0, g1, jnp.where(lane == 1, g2, 0.0))
    cnt_ref[...] = carry_ref[...].astype(jnp.int32)


def _router(x, w_router_padded, *, n_experts):
    n, d = x.shape
    tm = _tile(n, 512)
    vmem = 2 * (tm * d * 4 + d * V7X_LANES * 4 + 2 * tm * V7X_LANES * 4) + 8 * tm * tm * 4
    return pl.pallas_call(
        functools.partial(_router_kernel, n_experts=n_experts),
        grid=(n // tm,),
        in_specs=[pl.BlockSpec((tm, d), lambda i: (i, 0)),
                  pl.BlockSpec((d, V7X_LANES), lambda i: (0, 0))],
        out_specs=[pl.BlockSpec((tm, V7X_LANES), lambda i: (i, 0)),
                   pl.BlockSpec((tm, V7X_LANES), lambda i: (i, 0)),
                   pl.BlockSpec((1, V7X_LANES), lambda i: (0, 0))],
        out_shape=[jax.ShapeDtypeStruct((n, V7X_LANES), jnp.int32),
                   jax.ShapeDtypeStruct((n, V7X_LANES), F32),
                   jax.ShapeDtypeStruct((1, V7X_LANES), jnp.int32)],
        scratch_shapes=[pltpu.VMEM((1, V7X_LANES), F32)],
        compiler_params=_params(("arbitrary",), vmem),
        name="moe_router",
    )(x, w_router_padded)


def _dispatch_kernel(slot_ref, x_ref, init_hbm, xs_hbm, sem):
    del init_hbm
    chunk = x_ref.shape[0]

    def copies(r):
        return [pltpu.make_async_copy(x_ref.at[pl.ds(r, 1), :],
                                      xs_hbm.at[pl.ds(slot_ref[0, 0, TOP_K * r + k], 1), :], sem)
                for k in range(TOP_K)]

    def start(r, carry):
        for cp in copies(r):
            cp.start()
        return carry

    lax.fori_loop(0, chunk, start, 0, unroll=DMA_ISSUE_UNROLL)
    for _ in range(TOP_K):
        pltpu.make_async_copy(x_ref, xs_hbm.at[pl.ds(0, chunk), :], sem).wait()


def _dispatch(x, slots, *, n_slots):
    n, d = x.shape
    chunk = _tile(n, 512)
    slots3 = slots.reshape(n // chunk, 1, TOP_K * chunk)
    init = jnp.zeros((n_slots, d), x.dtype)
    return pl.pallas_call(
        _dispatch_kernel,
        grid=(n // chunk,),
        in_specs=[pl.BlockSpec((1, 1, TOP_K * chunk), lambda i: (i, 0, 0), memory_space=pltpu.SMEM),
                  pl.BlockSpec((chunk, d), lambda i: (i, 0)),
                  pl.BlockSpec(memory_space=pl.ANY)],
        out_specs=pl.BlockSpec(memory_space=pl.ANY),
        out_shape=jax.ShapeDtypeStruct((n_slots, d), x.dtype),
        scratch_shapes=[pltpu.SemaphoreType.DMA(())],
        input_output_aliases={2: 0},
        compiler_params=_params(("arbitrary",), 2 * chunk * d * 4),
        name="moe_dispatch",
    )(slots3, x, init)


def _moe_up_kernel(te_ref, rows_ref, xs_ref, wg_ref, wu_ref, o_ref):
    del te_ref
    t = pl.program_id(1)

    @pl.when(rows_ref[t] > 0)
    def _():
        x = jnp.concatenate(_unpack_bf16_pairs(xs_ref[...]), axis=1)
        gate = _dot(x, wg_ref[...].astype(BF16))
        up = _dot(x, wu_ref[...].astype(BF16))
        o_ref[...] = (jax.nn.silu(gate) * up).astype(o_ref.dtype)

    @pl.when(rows_ref[t] == 0)
    def _():
        o_ref[...] = jnp.zeros_like(o_ref)


def _moe_up(xs, w_gate, w_up, layer, tile_expert, tile_rows):
    s = xs.shape[0]
    d = 2 * xs.shape[1]
    f = w_gate.shape[3]
    tm = MOE_TILE
    tf = _tile(f, 1024)
    vmem = 2 * (tm * d * 2 + 2 * d * tf * 4 + tm * tf * 2) + 4 * tm * tf * 4
    w_spec = pl.BlockSpec((None, None, d, tf), lambda j, t, te, rows: (layer, te[t], 0, j))
    grid_spec = pltpu.PrefetchScalarGridSpec(
        num_scalar_prefetch=2,
        grid=(f // tf, s // tm),
        in_specs=[pl.BlockSpec((tm, d // 2), lambda j, t, te, rows: (t, 0)), w_spec, w_spec],
        out_specs=pl.BlockSpec((tm, tf), lambda j, t, te, rows: (t, j)),
    )
    return pl.pallas_call(
        _moe_up_kernel,
        grid_spec=grid_spec,
        out_shape=jax.ShapeDtypeStruct((s, f), BF16),
        compiler_params=_params(("arbitrary", "arbitrary"), vmem),
        name="moe_up",
    )(tile_expert, tile_rows, xs, w_gate, w_up)


def _moe_down_kernel(te_ref, rows_ref, h_hbm, wd_ref, o_ref, hbuf_ref, sem):
    del te_ref
    t = pl.program_id(1)
    n_t = pl.num_programs(1)
    total = pl.num_programs(0) * n_t
    step = pl.program_id(0) * n_t + t
    tm = hbuf_ref.shape[1]
    ahead = MOE_H_BUFFERS - 1

    def tile_copy(s):
        slot = s % MOE_H_BUFFERS
        return pltpu.make_async_copy(h_hbm.at[pl.ds((s % n_t) * tm, tm), :], hbuf_ref.at[slot], sem.at[slot])

    @pl.when(step == 0)
    def _():
        for s in range(ahead):
            tile_copy(s).start()

    @pl.when(step + ahead < total)
    def _():
        tile_copy(step + ahead).start()

    tile_copy(step).wait()

    @pl.when(rows_ref[t] > 0)
    def _():
        o_ref[...] = _dot(hbuf_ref[step % MOE_H_BUFFERS], wd_ref[...].astype(BF16))

    @pl.when(rows_ref[t] == 0)
    def _():
        o_ref[...] = jnp.zeros_like(o_ref)


def _moe_down(h, w_down, layer, tile_expert, tile_rows):
    s, f = h.shape
    d = w_down.shape[3]
    tm = MOE_TILE
    tn = _tile(d, 512)
    assert (d // tn) * (s // tm) >= MOE_H_BUFFERS
    vmem = MOE_H_BUFFERS * tm * f * 2 + 2 * (tm * tn * 4 + f * tn * 4) + 2 * tm * tn * 4
    grid_spec = pltpu.PrefetchScalarGridSpec(
        num_scalar_prefetch=2,
        grid=(d // tn, s // tm),
        in_specs=[pl.BlockSpec(memory_space=pl.ANY),
                  pl.BlockSpec((None, None, f, tn), lambda j, t, te, rows: (layer, te[t], 0, j))],
        out_specs=pl.BlockSpec((tm, tn), lambda j, t, te, rows: (t, j)),
        scratch_shapes=[pltpu.VMEM((MOE_H_BUFFERS, tm, f), BF16),
                        pltpu.SemaphoreType.DMA((MOE_H_BUFFERS,))],
    )
    return pl.pallas_call(
        _moe_down_kernel,
        grid_spec=grid_spec,
        out_shape=jax.ShapeDtypeStruct((s, d), F32),
        compiler_params=_params(("arbitrary", "arbitrary"), vmem),
        name="moe_down",
    )(tile_expert, tile_rows, h, w_down)


def _combine_ln_kernel(slot_ref, next_slot_ref, y_hbm, rg_ref, res_ref, g_ref, b_ref, of_ref, ob_ref,
                       buf_ref, sem, *, alpha):
    i = pl.program_id(0)
    tm = res_ref.shape[0]

    def gather(slots, parity):
        def start(r, carry):
            for k in range(TOP_K):
                pltpu.make_async_copy(y_hbm.at[pl.ds(slots[0, 0, TOP_K * r + k], 1), :],
                                      buf_ref.at[parity * TOP_K + k, pl.ds(r, 1), :],
                                      sem.at[parity]).start()
            return carry
        lax.fori_loop(0, tm, start, 0, unroll=DMA_ISSUE_UNROLL)

    @pl.when(i == 0)
    def _():
        gather(slot_ref, 0)

    @pl.when(i + 1 < pl.num_programs(0))
    def _():
        gather(next_slot_ref, (i + 1) % 2)

    parity = i % 2
    for k in range(TOP_K):
        pltpu.make_async_copy(y_hbm.at[pl.ds(0, tm), :], buf_ref.at[parity * TOP_K + k],
                              sem.at[parity]).wait()
    gates = rg_ref[...]
    ff = gates[:, 0:1] * buf_ref[parity * TOP_K] + gates[:, 1:2] * buf_ref[parity * TOP_K + 1]
    y = _layer_norm(alpha * res_ref[...] + ff, g_ref[...], b_ref[...])
    of_ref[...] = y
    ob_ref[...] = y.astype(BF16)


def _combine_ln(y, slots, gates, res, g, b, *, alpha):
    n, d = res.shape
    tm = _tile(n, 256)
    steps = n // tm
    slots3 = slots.reshape(steps, 1, TOP_K * tm)
    vmem = (2 * (tm * V7X_LANES * 4 + tm * d * 4 + tm * d * 6) + 2 * TOP_K * tm * d * 4
            + 3 * tm * d * 4)
    slot_block = (1, 1, TOP_K * tm)
    return pl.pallas_call(
        functools.partial(_combine_ln_kernel, alpha=alpha),
        grid=(steps,),
        in_specs=[pl.BlockSpec(slot_block, lambda i: (i, 0, 0), memory_space=pltpu.SMEM),
                  pl.BlockSpec(slot_block, lambda i: (jnp.minimum(i + 1, steps - 1), 0, 0),
                               memory_space=pltpu.SMEM),
                  pl.BlockSpec(memory_space=pl.ANY),
                  pl.BlockSpec((tm, V7X_LANES), lambda i: (i, 0)),
                  pl.BlockSpec((tm, d), lambda i: (i, 0)),
                  pl.BlockSpec((1, d), lambda i: (0, 0)),
                  pl.BlockSpec((1, d), lambda i: (0, 0))],
        out_specs=[pl.BlockSpec((tm, d), lambda i: (i, 0)),
                   pl.BlockSpec((tm, d), lambda i: (i, 0))],
        out_shape=[jax.ShapeDtypeStruct((n, d), F32), jax.ShapeDtypeStruct((n, d), BF16)],
        scratch_shapes=[pltpu.VMEM((2 * TOP_K, tm, d), F32), pltpu.SemaphoreType.DMA((2,))],
        compiler_params=_params(("arbitrary",), vmem),
        name="moe_combine_ln",
    )(slots3, slots3, y, gates, res, g, b)


def _moe_layer(x, x_packed, w_router, w_gate, w_up, w_down, layer, g, b, *, alpha):
    n, d = x.shape
    n_experts = w_router.shape[1]
    tm = MOE_TILE
    n_tiles = (TOP_K * n) // tm + n_experts
    wr = jnp.pad(w_router, ((0, 0), (0, V7X_LANES - n_experts)))
    ri, rg, cnt = _router(x, wr, n_experts=n_experts)

    counts = cnt[0, :n_experts]
    tiles_per = (counts + tm - 1) // tm
    tile_end = jnp.cumsum(tiles_per)
    n_used = tile_end[-1]
    group_start = (tile_end - tiles_per) * tm
    tile_id = jnp.arange(n_tiles, dtype=jnp.int32)
    last_used = jnp.minimum(tile_id, jnp.maximum(n_used - 1, 0))
    tile_expert = jnp.sum((last_used[:, None] >= tile_end[None, :]).astype(jnp.int32), axis=1)
    tile_expert = jnp.minimum(tile_expert, n_experts - 1)
    slots = (group_start[ri[:, :TOP_K]] + ri[:, TOP_K:2 * TOP_K]).astype(jnp.int32).reshape(-1)
    group_end = group_start + counts
    tile_rows = jnp.clip(group_end[tile_expert] - tile_id * tm, 0, tm)
    tile_rows = jnp.where(tile_id < n_used, tile_rows, 0).astype(jnp.int32)

    xs = _dispatch(x_packed, slots, n_slots=n_tiles * tm)
    h = _moe_up(xs, w_gate, w_up, layer, tile_expert, tile_rows)
    y = _moe_down(h, w_down, layer, tile_expert, tile_rows)
    return _combine_ln(y, slots, rg, x, g, b, alpha=alpha)


def kernel(x, conv_w_in, conv_w, conv_w_out, w_kv, attn_w_q, attn_sinks, attn_w_o, ln_g, ln_b,
           ffn_w_gate, ffn_w_up, ffn_w_down, moe_w_router, moe_w_gate, moe_w_up, moe_w_down):
    bsz, seq, d = x.shape
    n = bsz * seq
    depth = ln_g.shape[0]
    n_a = conv_w_in.shape[0]
    alpha = float((2 * depth) ** 0.25)
    assert seq % ATTN_BLOCK == 0 and d % (HEAD_DIM * GQA_GROUP) == 0
    assert w_kv.shape[1] == 2 * d // GQA_GROUP

    xf = x.reshape(n, d)
    xb = xf.astype(BF16)
    k_shared = vt_shared = None
    for l in range(depth):
        g0, b0 = ln_g[l, 0].reshape(1, d), ln_b[l, 0].reshape(1, d)
        g1, b1 = ln_g[l, 1].reshape(1, d), ln_b[l, 1].reshape(1, d)
        moe = l % 2 == 1
        if l < n_a:
            gated = _conv_gate(xb, conv_w_in, conv_w, l, seq=seq)
            xf, xb = _proj_ln(gated, conv_w_out[l].astype(BF16), xf, g0, b0, alpha=alpha,
                              name="conv_out_ln", packed=moe)
        else:
            j = l - n_a
            qt = _matmul_nt(attn_w_q[j].T.astype(BF16), xb, out_dtype=BF16,
                            scale=1.0 / math.sqrt(HEAD_DIM), name="attn_q")
            o = _attention(qt, k_shared, vt_shared, attn_sinks[j], bsz=bsz, seq=seq)
            xf, xb = _proj_ln(o, attn_w_o[j].astype(BF16), xf, g0, b0, alpha=alpha,
                              name="attn_out_ln", packed=moe)
        i = l // 2
        if moe:
            xf, xb = _moe_layer(xf, xb, moe_w_router[i], moe_w_gate, moe_w_up, moe_w_down, i,
                                g1, b1, alpha=alpha)
        else:
            h = _swiglu_up(xb, ffn_w_gate, ffn_w_up, i)
            xf, xb = _proj_ln(h, ffn_w_down[i].astype(BF16), xf, g1, b1, alpha=alpha, name="ffn_down_ln")
        if l == n_a - 1:
            kvd = w_kv.shape[1] // 2
            k_shared = _matmul(xb, w_kv[:, :kvd].astype(BF16), out_dtype=BF16, name="k_proj")
            vt_shared = _matmul_nt(w_kv[:, kvd:].T.astype(BF16), xb, out_dtype=BF16, name="v_proj")
    return xf.reshape(bsz, seq, d)
```
